```python
import jax, jax.numpy as jnp
from jax import lax
import numpy as np

D_MODEL = 1024
BATCH = 8
SEQ = 2048
DEPTH = 4
DEC_BATCH = 128
DEC_SEQ = 8
PAST_LEN = 16384
PAGE_SIZE = 128

N_EVEN = (DEPTH + 1) // 2
N_ODD = DEPTH // 2
POOL_WINDOWS = (2, 4, 8, 16)
N_POOL_GROUPS = len(POOL_WINDOWS)
D_POOL = D_MODEL // 2
POOL_GROUP = D_POOL // N_POOL_GROUPS
POOL_BUF = max(POOL_WINDOWS) - 1
D_CONF = D_MODEL // 2
CONF_WIDTH = 31
D_GCONV = D_MODEL
GCONV_WIDTH = 3
D_FF = 2816
FFN_CONV_WIDTH = 3
N_MEM = 256
N_MEM_HEADS = 4
MEM_HEAD_DIM = D_MODEL // N_MEM_HEADS
RMS_EPS = 1e-6
LN_EPS = 1e-5
NORM_MIX_PRE, NORM_MIX_POST, NORM_X_PRE, NORM_X_POST, NORM_FFN_PRE, NORM_FFN_POST, NORM_MEM = range(7)
N_NORMS = 7

kernel_name = 'hybrid_pool_conformer_shortconv_decoder_step'


def rms_norm(x, g):
    xf = x.astype(jnp.float32)
    y = xf * lax.rsqrt(jnp.mean(xf * xf, axis=-1, keepdims=True) + RMS_EPS)
    return (y * g.astype(jnp.float32)).astype(x.dtype)


def layer_norm(x, g, b):
    xf = x.astype(jnp.float32)
    mu = jnp.mean(xf, axis=-1, keepdims=True)
    xc = xf - mu
    var = jnp.mean(xc * xc, axis=-1, keepdims=True)
    y = xc * lax.rsqrt(var + LN_EPS) * g.astype(jnp.float32) + b.astype(jnp.float32)
    return y.astype(x.dtype)


def causal_dwconv(x_ext, w):
    c = w.shape[1]
    return lax.conv_general_dilated(x_ext, w[:, None, :].astype(x_ext.dtype), window_strides=(1,), padding='VALID',
                                    dimension_numbers=('NWC', 'WIO', 'NWC'), feature_group_count=c)


def multiscale_pool(a_ext, n_new, pos):
    L = a_ext.shape[1]
    af = a_ext.astype(jnp.float32)
    cs = jnp.cumsum(af, axis=1)
    outs = []
    for gi, w in enumerate(POOL_WINDOWS):
        sl = slice(gi * POOL_GROUP, (gi + 1) * POOL_GROUP)
        c = cs[..., sl]
        lagged = jnp.pad(c, ((0, 0), (w, 0), (0, 0)))[:, :L]
        win = (c - lagged)[:, L - n_new:]
        cnt = jnp.minimum(pos + 1, w).astype(jnp.float32)
        outs.append(win / cnt[None, :, None] - af[:, L - n_new:, sl])
    return jnp.stack(outs, axis=2).astype(a_ext.dtype)


def even_mixer(h, pool_prev, conf_prev, pos, w_in, w_pool, pool_scale, conf_w, conf_b, conf_ln_g, conf_ln_b, w_out):
    bsz, n, _ = h.shape
    p = h @ w_in
    a = p[..., :D_POOL]
    u = p[..., D_POOL:D_POOL + D_CONF] * jax.nn.sigmoid(p[..., D_POOL + D_CONF:])
    a_ext = jnp.concatenate([pool_prev, a], axis=1)
    u_ext = jnp.concatenate([conf_prev, u], axis=1)
    pooled = multiscale_pool(a_ext, n, pos)
    ya = jnp.einsum('bsgc,gcd->bsgd', pooled, w_pool).reshape(bsz, n, D_POOL) * pool_scale
    cb = causal_dwconv(u_ext, conf_w) + conf_b
    yb = jax.nn.silu(layer_norm(cb, conf_ln_g, conf_ln_b))
    y = jnp.concatenate([ya, yb], axis=-1) @ w_out
    return y, a_ext[:, -POOL_BUF:], u_ext[:, -(CONF_WIDTH - 1):]


def odd_mixer(h, gconv_prev, w_in, gconv_w, w_out):
    p = h @ w_in
    xin = p[..., :D_GCONV]
    gate_b = p[..., D_GCONV:2 * D_GCONV]
    gate_c = p[..., 2 * D_GCONV:]
    u_ext = jnp.concatenate([gconv_prev, gate_c * xin], axis=1)
    y = gate_b * causal_dwconv(u_ext, gconv_w)
    return y @ w_out, u_ext[:, -(GCONV_WIDTH - 1):]


def mem_kv(mem, g_mem, w_k, w_v):
    bsz = mem.shape[0]
    m = rms_norm(mem, g_mem)
    k = (m @ w_k).reshape(bsz, N_MEM, N_MEM_HEADS, MEM_HEAD_DIM)
    v = (m @ w_v).reshape(bsz, N_MEM, N_MEM_HEADS, MEM_HEAD_DIM)
    return k, v


def mem_attend(h, k, v, w_q, w_o):
    bsz, n, _ = h.shape
    q = (h @ w_q).reshape(bsz, n, N_MEM_HEADS, MEM_HEAD_DIM)
    s = jnp.einsum('bqhd,bkhd->bhqk', q, k).astype(jnp.float32) * (MEM_HEAD_DIM ** -0.5)
    pr = jax.nn.softmax(s, axis=-1).astype(v.dtype)
    o = jnp.einsum('bhqk,bkhd->bqhd', pr, v).reshape(bsz, n, D_MODEL)
    return o @ w_o


def conv_ffn(h, ffn_prev, w_gate, w_up, conv_w, conv_b, w_down):
    g_ext = jnp.concatenate([ffn_prev, h @ w_gate], axis=1)
    gc = causal_dwconv(g_ext, conv_w) + conv_b
    y = (jax.nn.silu(gc) * (h @ w_up)) @ w_down
    return y, g_ext[:, -(FFN_CONV_WIDTH - 1):]


def run_trunk(x, pos, pool_prev, conf_prev, gconv_prev, ffn_prev, mem_k, mem_v, prm):
    new_pool, new_conf, new_gconv, new_ffn = [], [], [], []
    for l in range(DEPTH):
        ng = prm['norm_gains'][l]
        h = rms_norm(x, ng[NORM_MIX_PRE])
        if l % 2 == 0:
            e = l // 2
            y, sp, sc = even_mixer(h, pool_prev[e], conf_prev[e], pos, prm['w_in_even'][e], prm['w_pool'][e],
                                   prm['pool_scale'][e], prm['conf_w'][e], prm['conf_b'][e], prm['conf_ln_g'][e],
                                   prm['conf_ln_b'][e], prm['w_out_even'][e])
            new_pool.append(sp)
            new_conf.append(sc)
        else:
            o = l // 2
            y, sg = odd_mixer(h, gconv_prev[o], prm['w_in_odd'][o], prm['gconv_w'][o], prm['w_out_odd'][o])
            new_gconv.append(sg)
        x = x + rms_norm(y, ng[NORM_MIX_POST])
        h = rms_norm(x, ng[NORM_X_PRE])
        y = mem_attend(h, mem_k[l], mem_v[l], prm['w_mem_q'][l], prm['w_mem_o'][l])
        x = x + rms_norm(y, ng[NORM_X_POST])
        h = rms_norm(x, ng[NORM_FFN_PRE])
        y, sf = conv_ffn(h, ffn_prev[l], prm['w_ffn_gate'][l], prm['w_ffn_up'][l], prm['ffn_conv_w'][l],
                         prm['ffn_conv_b'][l], prm['w_ffn_down'][l])
        new_ffn.append(sf)
        x = x + rms_norm(y, ng[NORM_FFN_POST])
    return x, jnp.stack(new_pool), jnp.stack(new_conf), jnp.stack(new_gconv), jnp.stack(new_ffn)


def setup_inputs(seed: int = 0) -> dict:
    key = jax.random.key(seed)
    ks = jax.random.split(key, 32)
    f32 = jnp.float32

    def nrm(k, shape, scale=1.0):
        return jax.random.normal(k, shape, f32) * scale

    return {
        'x_prompt': nrm(ks[0], (BATCH, SEQ, D_MODEL)),
        'x_sample': nrm(ks[1], (DEC_BATCH, DEC_SEQ, D_MODEL)),
        'state_pool': nrm(ks[2], (N_EVEN, DEC_BATCH, POOL_BUF, D_POOL)),
        'state_conf': nrm(ks[3], (N_EVEN, DEC_BATCH, CONF_WIDTH - 1, D_CONF)),
        'state_gconv': nrm(ks[4], (N_ODD, DEC_BATCH, GCONV_WIDTH - 1, D_GCONV)),
        'state_ffn': nrm(ks[5], (DEPTH, DEC_BATCH, FFN_CONV_WIDTH - 1, D_FF)),
        'cache_mem_k': nrm(ks[6], (DEPTH, DEC_BATCH, N_MEM, N_MEM_HEADS, MEM_HEAD_DIM)),
        'cache_mem_v': nrm(ks[7], (DEPTH, DEC_BATCH, N_MEM, N_MEM_HEADS, MEM_HEAD_DIM)),
        'mem_prompt': nrm(ks[8], (BATCH, N_MEM, D_MODEL)),
        'norm_gains': 1.0 + nrm(ks[9], (DEPTH, N_NORMS, D_MODEL), 0.05),
        'w_in_even': nrm(ks[10], (N_EVEN, D_MODEL, D_POOL + 2 * D_CONF), D_MODEL ** -0.5),
        'w_pool': nrm(ks[11], (N_EVEN, N_POOL_GROUPS, POOL_GROUP, POOL_GROUP), POOL_GROUP ** -0.5),
        'pool_scale': 1.0 + nrm(ks[12], (N_EVEN, D_POOL), 0.1),
        'conf_w': nrm(ks[13], (N_EVEN, CONF_WIDTH, D_CONF), CONF_WIDTH ** -0.5),
        'conf_b': nrm(ks[14], (N_EVEN, D_CONF), 0.02),
        'conf_ln_g': 1.0 + nrm(ks[15], (N_EVEN, D_CONF), 0.05),
        'conf_ln_b': nrm(ks[16], (N_EVEN, D_CONF), 0.02),
        'w_out_even': nrm(ks[17], (N_EVEN, D_POOL + D_CONF, D_MODEL), (D_POOL + D_CONF) ** -0.5),
        'w_in_odd': nrm(ks[18], (N_ODD, D_MODEL, 3 * D_GCONV), D_MODEL ** -0.5),
        'gconv_w': nrm(ks[19], (N_ODD, GCONV_WIDTH, D_GCONV), GCONV_WIDTH ** -0.5),
        'w_out_odd': nrm(ks[20], (N_ODD, D_GCONV, D_MODEL), D_GCONV ** -0.5),
        'w_mem_q': nrm(ks[21], (DEPTH, D_MODEL, D_MODEL), D_MODEL ** -0.5),
        'w_mem_k': nrm(ks[22], (DEPTH, D_MODEL, D_MODEL), D_MODEL ** -0.5),
        'w_mem_v': nrm(ks[23], (DEPTH, D_MODEL, D_MODEL), D_MODEL ** -0.5),
        'w_mem_o': nrm(ks[24], (DEPTH, D_MODEL, D_MODEL), D_MODEL ** -0.5),
        'w_ffn_gate': nrm(ks[25], (DEPTH, D_MODEL, D_FF), D_MODEL ** -0.5),
        'w_ffn_up': nrm(ks[26], (DEPTH, D_MODEL, D_FF), D_MODEL ** -0.5),
        'ffn_conv_w': nrm(ks[27], (DEPTH, FFN_CONV_WIDTH, D_FF), FFN_CONV_WIDTH ** -0.5),
        'ffn_conv_b': nrm(ks[28], (DEPTH, D_FF), 0.02),
        'w_ffn_down': nrm(ks[29], (DEPTH, D_FF, D_MODEL), D_FF ** -0.5),
    }


def reference(x_prompt, x_sample, state_pool, state_conf, state_gconv, state_ffn, cache_mem_k, cache_mem_v,
              mem_prompt, norm_gains, w_in_even, w_pool, pool_scale, conf_w, conf_b, conf_ln_g, conf_ln_b,
              w_out_even, w_in_odd, gconv_w, w_out_odd, w_mem_q, w_mem_k, w_mem_v, w_mem_o,
              w_ffn_gate, w_ffn_up, ffn_conv_w, ffn_conv_b, w_ffn_down):
    prm = dict(norm_gains=norm_gains, w_in_even=w_in_even, w_pool=w_pool, pool_scale=pool_scale, conf_w=conf_w,
               conf_b=conf_b, conf_ln_g=conf_ln_g, conf_ln_b=conf_ln_b, w_out_even=w_out_even, w_in_odd=w_in_odd,
               gconv_w=gconv_w, w_out_odd=w_out_odd, w_mem_q=w_mem_q, w_mem_o=w_mem_o, w_ffn_gate=w_ffn_gate,
               w_ffn_up=w_ffn_up, ffn_conv_w=ffn_conv_w, ffn_conv_b=ffn_conv_b, w_ffn_down=w_ffn_down)
    dt = x_prompt.dtype
    kv = [mem_kv(mem_prompt, norm_gains[l, NORM_MEM], w_mem_k[l], w_mem_v[l]) for l in range(DEPTH)]
    mem_k_p = jnp.stack([k for k, _ in kv])
    mem_v_p = jnp.stack([v for _, v in kv])
    pos_p = jnp.arange(SEQ, dtype=jnp.int32)
    y_prompt, pool_p, conf_p, gconv_p, ffn_p = run_trunk(
        x_prompt, pos_p,
        jnp.zeros((N_EVEN, BATCH, POOL_BUF, D_POOL), dt),
        jnp.zeros((N_EVEN, BATCH, CONF_WIDTH - 1, D_CONF), dt),
        jnp.zeros((N_ODD, BATCH, GCONV_WIDTH - 1, D_GCONV), dt),
        jnp.zeros((DEPTH, BATCH, FFN_CONV_WIDTH - 1, D_FF), dt),
        mem_k_p, mem_v_p, prm)
    pos_s = PAST_LEN + jnp.arange(DEC_SEQ, dtype=jnp.int32)
    y_sample, pool_s, conf_s, gconv_s, ffn_s = run_trunk(
        x_sample, pos_s, state_pool, state_conf, state_gconv, state_ffn, cache_mem_k, cache_mem_v, prm)
    return (y_prompt, y_sample, pool_p, conf_p, gconv_p, ffn_p, mem_k_p, mem_v_p, pool_s, conf_s, gconv_s, ffn_s)
```

```python
import functools

import jax
import jax.numpy as jnp
from jax import lax
from jax.experimental import pallas as pl
from jax.experimental.pallas import tpu as pltpu

D_MODEL = 1024
BATCH = 8
SEQ = 2048
DEPTH = 4
DEC_BATCH = 128
DEC_SEQ = 8
PAST_LEN = 16384
N_EVEN = (DEPTH + 1) // 2
N_ODD = DEPTH // 2
POOL_WINDOWS = (2, 4, 8, 16)
D_POOL = 512
POOL_BUF = 15
D_CONF = 512
CONF_WIDTH = 31
D_GCONV = 1024
GCONV_WIDTH = 3
D_FF = 2816
FFN_CONV_WIDTH = 3
N_MEM = 256
N_MEM_HEADS = 4
MEM_HEAD_DIM = 256
RMS_EPS = 1e-6
LN_EPS = 1e-5
NORM_MIX_PRE, NORM_MIX_POST, NORM_X_PRE, NORM_X_POST, NORM_FFN_PRE, NORM_FFN_POST, NORM_MEM = range(7)

F32 = jnp.float32
BF16 = jnp.bfloat16

LANES = 128
SUBLANES = 8
VMEM_LIMIT_BYTES = 56 * 1024 * 1024

POOL_HIST = 16
CONF_HIST = 32
SHORT_HIST = 8

FFN_CHUNKS = ((0, 768), (768, 1536), (1536, 2304), (2304, 2816))

PROMPT_TILE = 512
SAMPLE_SEQS = 32
ATTN_SAMPLE_SEQS = 4
CONV_ROWS = 128

MEM_ROWS = N_MEM_HEADS * MEM_HEAD_DIM // LANES
HALF_HEAD = MEM_HEAD_DIM // LANES


def _rms(x, g):
    ms = jnp.mean(x * x, axis=-1, keepdims=True)
    return x * lax.rsqrt(ms + RMS_EPS) * g


def _mm(a, w):
    return jnp.dot(a.astype(BF16), w, preferred_element_type=F32)


def _lanes(i):
    return slice(i * LANES, (i + 1) * LANES)


def _seq_view(scr, bs):
    n_blk, total, _ = scr.shape
    return scr.reshape(n_blk, bs, total // bs, LANES)


def _carry_history(scr, bs, hist, ts, nj):
    if nj > 1:
        view = _seq_view(scr, bs)

        @pl.when(pl.program_id(1) > 0)
        def _():
            for g in range(scr.shape[0]):
                view[g, :, 0:hist, :] = view[g, :, ts:ts + hist, :]


def _first_tile(fn):
    pl.when(pl.program_id(1) == 0)(fn)


def _load_tm_state(scr, st_ref, bs, hist, n_state, ts):
    rows = hist + ts
    if st_ref is None:
        view = _seq_view(scr, bs)

        def fill():
            for g in range(scr.shape[0]):
                view[g, :, 0:hist, :] = jnp.zeros((bs, hist, LANES), F32)
    else:
        def fill():
            for g in range(scr.shape[0]):
                for r in range(n_state):
                    scr[g, pl.ds(hist - n_state + r, bs, stride=rows), :] = st_ref[r, :, _lanes(g)]
    _first_tile(fill)


def _store_tm_state(ns_ref, scr, bs, hist, n_state, ts):
    rows = hist + ts
    for g in range(scr.shape[0]):
        for r in range(n_state):
            ns_ref[r, :, _lanes(g)] = scr[g, pl.ds(ts + hist - n_state + r, bs, stride=rows), :]


def _store_tail(ns_ref, scr, bs, hist, ts):
    view = _seq_view(scr, bs)
    for g in range(scr.shape[0]):
        ns_ref[:, :, _lanes(g)] = view[g, :, ts:ts + hist, :]


def _load_bm_state(scr, st_ref, bs, hist, n_state):
    view = _seq_view(scr, bs)

    def fill():
        for g in range(scr.shape[0]):
            if st_ref is None:
                view[g, :, hist - n_state:hist, :] = jnp.zeros((bs, n_state, LANES), F32)
            else:
                view[g, :, hist - n_state:hist, :] = st_ref[:, :, _lanes(g)]
    _first_tile(fill)


def _even_kernel(*refs, layer, bs, ts, nj, pos0, has_state):
    if has_state:
        x_ref, sp_ref, sc_ref, *refs = refs
    else:
        x_ref, *refs = refs
        sp_ref = sc_ref = None
    (gains_ref, w_in_ref, w_pool_ref, pscale_ref, cw_ref, cb_ref, lng_ref, lnb_ref, w_out_ref,
     xo_ref, npool_ref, nconf_ref, a_scr, u_scr, cv_scr) = refs
    m = bs * ts
    n_blk = D_POOL // LANES
    e = layer // 2
    _load_tm_state(a_scr, sp_ref, bs, POOL_HIST, POOL_BUF, ts)
    _load_tm_state(u_scr, sc_ref, bs, CONF_HIST, CONF_WIDTH - 1, ts)
    _carry_history(a_scr, bs, POOL_HIST, ts, nj)
    _carry_history(u_scr, bs, CONF_HIST, ts, nj)
    a_view = _seq_view(a_scr, bs)
    u_view = _seq_view(u_scr, bs)

    x = x_ref[...]
    h = _rms(x, gains_ref[NORM_MIX_PRE, layer:layer + 1, :])
    p = _mm(h, w_in_ref[...])
    u = p[:, D_POOL:D_POOL + D_CONF] * jax.nn.sigmoid(p[:, D_POOL + D_CONF:])
    for g in range(n_blk):
        a_view[g, :, POOL_HIST:POOL_HIST + ts, :] = p[:, _lanes(g)].reshape(bs, ts, LANES)
        u_view[g, :, CONF_HIST:CONF_HIST + ts, :] = u[:, _lanes(g)].reshape(bs, ts, LANES)

    pos = pos0 + pl.program_id(1) * ts + lax.broadcasted_iota(jnp.int32, (1, ts, LANES), 1)
    ya = []
    for g, w in enumerate(POOL_WINDOWS):
        cur = a_view[g, :, POOL_HIST:POOL_HIST + ts, :]
        win = cur
        for i in range(1, w):
            win = win + a_view[g, :, POOL_HIST - i:POOL_HIST - i + ts, :]
        inv_cnt = 1.0 / jnp.minimum(pos + 1, w).astype(F32)
        pooled = win * inv_cnt - cur
        ya.append(_mm(pooled.reshape(m, LANES), w_pool_ref[g]))
    ya = jnp.concatenate(ya, axis=-1) * pscale_ref[e:e + 1, :]

    if ts >= CONV_ROWS:
        bb, rb = 1, CONV_ROWS
    else:
        bb, rb = CONV_ROWS // ts, ts
    base = CONF_HIST - (CONF_WIDTH - 1)
    for c in range(n_blk):
        for b0 in range(0, bs, bb):
            for r0 in range(0, ts, rb):
                acc = None
                for k in range(CONF_WIDTH):
                    term = (u_view[c, b0:b0 + bb, base + k + r0:base + k + r0 + rb, :]
                            * cw_ref[k, e:e + 1, _lanes(c)].reshape(1, 1, LANES))
                    acc = term if acc is None else acc + term
                cv_scr[b0:b0 + bb, r0:r0 + rb, _lanes(c)] = acc
    cb = cv_scr[...].reshape(m, D_CONF) + cb_ref[e:e + 1, :]
    mu = jnp.mean(cb, axis=-1, keepdims=True)
    xc = cb - mu
    var = jnp.mean(xc * xc, axis=-1, keepdims=True)
    ln = xc * lax.rsqrt(var + LN_EPS) * lng_ref[e:e + 1, :] + lnb_ref[e:e + 1, :]
    yb = ln * jax.nn.sigmoid(ln)

    y = _mm(jnp.concatenate([ya, yb], axis=-1), w_out_ref[...])
    xo_ref[...] = x + _rms(y, gains_ref[NORM_MIX_POST, layer:layer + 1, :])

    if has_state:
        _store_tm_state(npool_ref, a_scr, bs, POOL_HIST, POOL_BUF, ts)
        _store_tm_state(nconf_ref, u_scr, bs, CONF_HIST, CONF_WIDTH - 1, ts)
    else:
        _store_tail(npool_ref, a_scr, bs, POOL_HIST, ts)
        _store_tail(nconf_ref, u_scr, bs, CONF_HIST, ts)


def _odd_kernel(*refs, layer, bs, ts, nj, has_state):
    if has_state:
        x_ref, sg_ref, *refs = refs
    else:
        x_ref, *refs = refs
        sg_ref = None
    gains_ref, w_in_ref, gw_ref, w_out_ref, xo_ref, ng_ref, v_scr = refs
    m = bs * ts
    n_blk = D_GCONV // LANES
    n_state = GCONV_WIDTH - 1
    o = layer // 2
    _load_bm_state(v_scr, sg_ref, bs, SHORT_HIST, n_state)
    _carry_history(v_scr, bs, SHORT_HIST, ts, nj)
    view = _seq_view(v_scr, bs)

    x = x_ref[...]
    h = _rms(x, gains_ref[NORM_MIX_PRE, layer:layer + 1, :])
    p = _mm(h, w_in_ref[...])
    v = p[:, 2 * D_GCONV:] * p[:, :D_GCONV]
    ys = []
    for g in range(n_blk):
        view[g, :, SHORT_HIST:SHORT_HIST + ts, :] = v[:, _lanes(g)].reshape(bs, ts, LANES)
        conv = None
        for k in range(GCONV_WIDTH):
            r = SHORT_HIST - n_state + k
            term = view[g, :, r:r + ts, :] * gw_ref[k, o:o + 1, _lanes(g)].reshape(1, 1, LANES)
            conv = term if conv is None else conv + term
        ys.append(conv.reshape(m, LANES))
    y = p[:, D_GCONV:2 * D_GCONV] * jnp.concatenate(ys, axis=-1)
    out = _mm(y, w_out_ref[...])
    xo_ref[...] = x + _rms(out, gains_ref[NORM_MIX_POST, layer:layer + 1, :])
    for g in range(n_blk):
        ng_ref[:, :, _lanes(g)] = view[g, :, ts + SHORT_HIST - n_state:ts + SHORT_HIST, :]


def _ffn_kernel(*refs, layer, bs, ts, nj, has_state):
    if has_state:
        x_ref, sf_ref, *refs = refs
    else:
        x_ref, *refs = refs
        sf_ref = None
    gains_ref, wg_ref, wu_ref, cw_ref, cb_ref, wd_ref, xo_ref, nf_ref, g_scr = refs
    m = bs * ts
    n_blk = D_FF // LANES
    n_state = FFN_CONV_WIDTH - 1
    _load_bm_state(g_scr, sf_ref, bs, SHORT_HIST, n_state)
    _carry_history(g_scr, bs, SHORT_HIST, ts, nj)
    view = _seq_view(g_scr, bs)

    x = x_ref[...]
    h = _rms(x, gains_ref[NORM_FFN_PRE, layer:layer + 1, :]).astype(BF16)
    acc = None
    for c0, c1 in FFN_CHUNKS:
        gate = jnp.dot(h, wg_ref[:, c0:c1], preferred_element_type=F32)
        up = jnp.dot(h, wu_ref[:, c0:c1], preferred_element_type=F32)
        gcs = []
        for l in range(c0 // LANES, c1 // LANES):
            off = l * LANES - c0
            view[l, :, SHORT_HIST:SHORT_HIST + ts, :] = gate[:, off:off + LANES].reshape(bs, ts, LANES)
            conv = None
            for k in range(FFN_CONV_WIDTH):
                r = SHORT_HIST - n_state + k
                term = view[l, :, r:r + ts, :] * cw_ref[k, layer:layer + 1, _lanes(l)].reshape(1, 1, LANES)
                conv = term if conv is None else conv + term
            gcs.append(conv.reshape(m, LANES))
        gc = jnp.concatenate(gcs, axis=-1) + cb_ref[layer:layer + 1, c0:c1]
        act = gc * jax.nn.sigmoid(gc) * up
        part = _mm(act, wd_ref[c0:c1, :])
        acc = part if acc is None else acc + part
    xo_ref[...] = x + _rms(acc, gains_ref[NORM_FFN_POST, layer:layer + 1, :])
    for l in range(n_blk):
        nf_ref[:, :, _lanes(l)] = view[l, :, ts + SHORT_HIST - n_state:ts + SHORT_HIST, :]


def _softmax_rows(s):
    e = jnp.exp(s - jnp.max(s, axis=-1, keepdims=True))
    return e * (1.0 / jnp.sum(e, axis=-1, keepdims=True))


def _attend(q, k_of_head, v_of_head):
    outs = []
    for hd in range(N_MEM_HEADS):
        cols = slice(hd * MEM_HEAD_DIM, (hd + 1) * MEM_HEAD_DIM)
        s = lax.dot_general(q[:, cols], k_of_head(hd), (((1,), (1,)), ((), ())),
                            preferred_element_type=F32)
        outs.append(_mm(_softmax_rows(s), v_of_head(hd)))
    return jnp.concatenate(outs, axis=-1)


ATTN_SCALE = MEM_HEAD_DIM ** -0.5


def _head_cols(hd):
    return slice(hd * MEM_HEAD_DIM, (hd + 1) * MEM_HEAD_DIM)


def _attn_prompt_kernel(x_ref, k_ref, v_ref, gains_ref, wq_ref, wo_ref, xo_ref, *, layer):
    x = x_ref[...]
    h = _rms(x, gains_ref[NORM_X_PRE, layer:layer + 1, :])
    q = (_mm(h, wq_ref[...]) * ATTN_SCALE).astype(BF16)
    o = _attend(q, lambda hd: k_ref[:, _head_cols(hd)], lambda hd: v_ref[:, _head_cols(hd)])
    y = _mm(o, wo_ref[...])
    xo_ref[...] = x + _rms(y, gains_ref[NORM_X_POST, layer:layer + 1, :])


def _kv_kernel(mem_ref, gains_ref, wk_ref, wv_ref, k32_ref, v32_ref, kb_ref, vb_ref):
    layer = pl.program_id(0)
    rows = mem_ref.shape[0]
    mn = _rms(mem_ref[...], gains_ref[NORM_MEM, pl.ds(layer, 1), :]).astype(BF16)
    for w_ref, o32_ref, ob_ref in ((wk_ref, k32_ref, kb_ref), (wv_ref, v32_ref, vb_ref)):
        kv = jnp.dot(mn, w_ref[...], preferred_element_type=F32)
        ob_ref[...] = kv.astype(BF16)
        for hd in range(N_MEM_HEADS):
            for half in range(HALF_HEAD):
                c0 = hd * MEM_HEAD_DIM + half * LANES
                o32_ref[pl.ds(half * N_MEM_HEADS + hd, rows, stride=MEM_ROWS), :] = kv[:, c0:c0 + LANES]


def _qproj_kernel(x_ref, gains_ref, wq_ref, q_ref, *, layer):
    h = _rms(x_ref[...], gains_ref[NORM_X_PRE, layer:layer + 1, :])
    q_ref[...] = _mm(h, wq_ref[...]) * ATTN_SCALE


def _cache_head(c_ref, b, hd):
    halves = [c_ref[b, pl.ds(half * N_MEM_HEADS + hd, N_MEM, stride=MEM_ROWS), :] for half in range(HALF_HEAD)]
    return jnp.concatenate(halves, axis=-1).astype(BF16)


def _attn_core_kernel(q_ref, k_ref, v_ref, o_ref, *, bs, ts):
    for b in range(bs):
        rows = slice(b * ts, (b + 1) * ts)
        q = q_ref[rows, :].astype(BF16)
        o_ref[rows, :] = _attend(q, functools.partial(_cache_head, k_ref, b),
                                 functools.partial(_cache_head, v_ref, b))


def _oproj_kernel(x_ref, o_ref, gains_ref, wo_ref, xo_ref, *, layer):
    y = _mm(o_ref[...], wo_ref[...])
    xo_ref[...] = x_ref[...] + _rms(y, gains_ref[NORM_X_POST, layer:layer + 1, :])


def _params(n_axes):
    return pltpu.CompilerParams(dimension_semantics=("arbitrary",) * n_axes,
                                vmem_limit_bytes=VMEM_LIMIT_BYTES)


def _whole(arr):
    nd = arr.ndim
    return pl.BlockSpec(arr.shape, lambda *_: (0,) * nd)


def _layer_weight(arr, layer):
    nd = arr.ndim
    return pl.BlockSpec((None,) + arr.shape[1:], lambda *_: (layer,) + (0,) * (nd - 1),
                        pipeline_mode=pl.Buffered(1))


def _row_spec(m, nj, width):
    return pl.BlockSpec((m, width), lambda b, j: (b * nj + j, 0))


def _tm_state_in(idx, n_state, bs, width):
    return pl.BlockSpec((None, n_state, bs, width), lambda b, j: (idx, 0, b, 0))


def _tm_state_out(n_state, bs, width):
    return pl.BlockSpec((n_state, bs, width), lambda b, j: (0, b, 0))


def _bm_state_in(idx, n_state, bs, width):
    return pl.BlockSpec((None, bs, n_state, width), lambda b, j: (idx, b, 0, 0))


def _bm_state_out(n_state, bs, width):
    return pl.BlockSpec((bs, n_state, width), lambda b, j: (b, 0, 0))


def _even_mixer(x, states, prm, layer, *, nseq, bs, ts, pos0):
    nb, nj = nseq // bs, x.shape[0] // (nseq * ts)
    m = bs * ts
    e = layer // 2
    has_state = states is not None
    kern = functools.partial(_even_kernel, layer=layer, bs=bs, ts=ts, nj=nj, pos0=pos0, has_state=has_state)
    small = [prm["gains"], prm["pool_scale"], prm["conf_w"], prm["conf_b"], prm["conf_ln_g"], prm["conf_ln_b"]]
    args = [x]
    in_specs = [_row_spec(m, nj, D_MODEL)]
    if has_state:
        args += [states["pool"], states["conf"]]
        in_specs += [_tm_state_in(e, POOL_BUF, bs, D_POOL), _tm_state_in(e, CONF_WIDTH - 1, bs, D_CONF)]
    args += [small[0], prm["w_in_even"], prm["w_pool"], *small[1:], prm["w_out_even"]]
    in_specs += [_whole(small[0]), _layer_weight(prm["w_in_even"], e), _layer_weight(prm["w_pool"], e),
                 *[_whole(a) for a in small[1:]], _layer_weight(prm["w_out_even"], e)]
    if has_state:
        state_specs = [_tm_state_out(POOL_BUF, bs, D_POOL), _tm_state_out(CONF_WIDTH - 1, bs, D_CONF)]
        state_shapes = [(POOL_BUF, nseq, D_POOL), (CONF_WIDTH - 1, nseq, D_CONF)]
    else:
        state_specs = [_bm_state_out(POOL_HIST, bs, D_POOL), _bm_state_out(CONF_HIST, bs, D_CONF)]
        state_shapes = [(nseq, POOL_HIST, D_POOL), (nseq, CONF_HIST, D_CONF)]
    return pl.pallas_call(
        kern,
        grid=(nb, nj),
        in_specs=in_specs,
        out_specs=[_row_spec(m, nj, D_MODEL), *state_specs],
        out_shape=[jax.ShapeDtypeStruct(x.shape, F32), *[jax.ShapeDtypeStruct(s, F32) for s in state_shapes]],
        scratch_shapes=[pltpu.VMEM((D_POOL // LANES, bs * (POOL_HIST + ts), LANES), F32),
                        pltpu.VMEM((D_CONF // LANES, bs * (CONF_HIST + ts), LANES), F32),
                        pltpu.VMEM((bs, ts, D_CONF), F32)],
        compiler_params=_params(2),
        name="even_mixer",
    )(*args)


def _odd_mixer(x, states, prm, layer, *, nseq, bs, ts):
    nb, nj = nseq // bs, x.shape[0] // (nseq * ts)
    m = bs * ts
    o = layer // 2
    has_state = states is not None
    kern = functools.partial(_odd_kernel, layer=layer, bs=bs, ts=ts, nj=nj, has_state=has_state)
    args = [x]
    in_specs = [_row_spec(m, nj, D_MODEL)]
    if has_state:
        args.append(states["gconv"])
        in_specs.append(_bm_state_in(o, GCONV_WIDTH - 1, bs, D_GCONV))
    args += [prm["gains"], prm["w_in_odd"], prm["gconv_w"], prm["w_out_odd"]]
    in_specs += [_whole(prm["gains"]), _layer_weight(prm["w_in_odd"], o), _whole(prm["gconv_w"]),
                 _layer_weight(prm["w_out_odd"], o)]
    return pl.pallas_call(
        kern,
        grid=(nb, nj),
        in_specs=in_specs,
        out_specs=[_row_spec(m, nj, D_MODEL), _bm_state_out(GCONV_WIDTH - 1, bs, D_GCONV)],
        out_shape=[jax.ShapeDtypeStruct(x.shape, F32),
                   jax.ShapeDtypeStruct((nseq, GCONV_WIDTH - 1, D_GCONV), F32)],
        scratch_shapes=[pltpu.VMEM((D_GCONV // LANES, bs * (SHORT_HIST + ts), LANES), F32)],
        compiler_params=_params(2),
        name="odd_mixer",
    )(*args)


def _conv_ffn(x, states, prm, layer, *, nseq, bs, ts):
    nb, nj = nseq // bs, x.shape[0] // (nseq * ts)
    m = bs * ts
    has_state = states is not None
    kern = functools.partial(_ffn_kernel, layer=layer, bs=bs, ts=ts, nj=nj, has_state=has_state)
    args = [x]
    in_specs = [_row_spec(m, nj, D_MODEL)]
    if has_state:
        args.append(states["ffn"])
        in_specs.append(_bm_state_in(layer, FFN_CONV_WIDTH - 1, bs, D_FF))
    args += [prm["gains"], prm["w_ffn_gate"], prm["w_ffn_up"], prm["ffn_conv_w"], prm["ffn_conv_b"],
             prm["w_ffn_down"]]
    in_specs += [_whole(prm["gains"]), _layer_weight(prm["w_ffn_gate"], layer),
                 _layer_weight(prm["w_ffn_up"], layer), _whole(prm["ffn_conv_w"]), _whole(prm["ffn_conv_b"]),
                 _layer_weight(prm["w_ffn_down"], layer)]
    return pl.pallas_call(
        kern,
        grid=(nb, nj),
        in_specs=in_specs,
        out_specs=[_row_spec(m, nj, D_MODEL), _bm_state_out(FFN_CONV_WIDTH - 1, bs, D_FF)],
        out_shape=[jax.ShapeDtypeStruct(x.shape, F32),
                   jax.ShapeDtypeStruct((nseq, FFN_CONV_WIDTH - 1, D_FF), F32)],
        scratch_shapes=[pltpu.VMEM((D_FF // LANES, bs * (SHORT_HIST + ts), LANES), F32)],
        compiler_params=_params(2),
        name="conv_ffn",
    )(*args)


def _attn_prompt(x, kb, vb, prm, layer, *, ts):
    nj = SEQ // ts
    kv_spec = pl.BlockSpec((None, N_MEM, D_MODEL), lambda b, j: (layer * BATCH + b, 0, 0))
    return pl.pallas_call(
        functools.partial(_attn_prompt_kernel, layer=layer),
        grid=(BATCH, nj),
        in_specs=[_row_spec(ts, nj, D_MODEL), kv_spec, kv_spec, _whole(prm["gains"]),
                  _layer_weight(prm["w_mem_q"], layer), _layer_weight(prm["w_mem_o"], layer)],
        out_specs=_row_spec(ts, nj, D_MODEL),
        out_shape=jax.ShapeDtypeStruct(x.shape, F32),
        compiler_params=_params(2),
        name="attn_prompt",
    )(x, kb, vb, prm["gains"], prm["w_mem_q"], prm["w_mem_o"])


def _mem_kv(mem, gains, wk, wv, *, tile):
    rows = mem.shape[0]
    w_spec = pl.BlockSpec((None, D_MODEL, D_MODEL), lambda l, i: (l, 0, 0))
    o32 = pl.BlockSpec((None, tile * MEM_ROWS, LANES), lambda l, i: (l, i, 0))
    ob = pl.BlockSpec((None, tile, D_MODEL), lambda l, i: (l, i, 0))
    return pl.pallas_call(
        _kv_kernel,
        grid=(DEPTH, rows // tile),
        in_specs=[pl.BlockSpec((tile, D_MODEL), lambda l, i: (i, 0)), _whole(gains), w_spec, w_spec],
        out_specs=[o32, o32, ob, ob],
        out_shape=[jax.ShapeDtypeStruct((DEPTH, rows * MEM_ROWS, LANES), F32)] * 2
        + [jax.ShapeDtypeStruct((DEPTH, rows, D_MODEL), BF16)] * 2,
        compiler_params=_params(2),
        name="mem_kv",
    )(mem, gains, wk, wv)


def _attn_sample(x, ck, cv, prm, layer, *, tile, bs):
    rows = x.shape[0]
    gains = prm["gains"]
    row1 = pl.BlockSpec((tile, D_MODEL), lambda i: (i, 0))
    q = pl.pallas_call(
        functools.partial(_qproj_kernel, layer=layer),
        grid=(rows // tile,),
        in_specs=[row1, _whole(gains), _layer_weight(prm["w_mem_q"], layer)],
        out_specs=row1,
        out_shape=jax.ShapeDtypeStruct((rows, D_MODEL), F32),
        compiler_params=_params(1),
        name="attn_q",
    )(x, gains, prm["w_mem_q"])
    rowc = pl.BlockSpec((bs * DEC_SEQ, D_MODEL), lambda i: (i, 0))
    kv_spec = pl.BlockSpec((None, bs, N_MEM * MEM_ROWS, LANES), lambda i: (layer, i, 0, 0))
    o = pl.pallas_call(
        functools.partial(_attn_core_kernel, bs=bs, ts=DEC_SEQ),
        grid=(DEC_BATCH // bs,),
        in_specs=[rowc, kv_spec, kv_spec],
        out_specs=rowc,
        out_shape=jax.ShapeDtypeStruct((rows, D_MODEL), F32),
        compiler_params=_params(1),
        name="attn_core",
    )(q, ck, cv)
    return pl.pallas_call(
        functools.partial(_oproj_kernel, layer=layer),
        grid=(rows // tile,),
        in_specs=[row1, row1, _whole(gains), _layer_weight(prm["w_mem_o"], layer)],
        out_specs=row1,
        out_shape=jax.ShapeDtypeStruct((rows, D_MODEL), F32),
        compiler_params=_params(1),
        name="attn_o",
    )(x, o, gains, prm["w_mem_o"])


def _trunk(x, states, attend, prm, *, nseq, bs, ts, pos0):
    new_pool, new_conf, new_gconv, new_ffn = [], [], [], []
    for l in range(DEPTH):
        if l % 2 == 0:
            x, sp, sc = _even_mixer(x, states, prm, l, nseq=nseq, bs=bs, ts=ts, pos0=pos0)
            new_pool.append(sp)
            new_conf.append(sc)
        else:
            x, sg = _odd_mixer(x, states, prm, l, nseq=nseq, bs=bs, ts=ts)
            new_gconv.append(sg)
        x = attend(x, l)
        x, sf = _conv_ffn(x, states, prm, l, nseq=nseq, bs=bs, ts=ts)
        new_ffn.append(sf)
    new_pool, new_conf = jnp.stack(new_pool), jnp.stack(new_conf)
    if states is None:
        new_pool = new_pool[:, :, POOL_HIST - POOL_BUF:, :]
        new_conf = new_conf[:, :, CONF_HIST - (CONF_WIDTH - 1):, :]
    else:
        new_pool, new_conf = new_pool.transpose(0, 2, 1, 3), new_conf.transpose(0, 2, 1, 3)
    return x, new_pool, new_conf, jnp.stack(new_gconv), jnp.stack(new_ffn)


def _cache_rows(c):
    d, n = c.shape[0], c.shape[1]
    c = c.reshape(d, n, N_MEM, N_MEM_HEADS, HALF_HEAD, LANES).transpose(0, 1, 2, 4, 3, 5)
    return c.reshape(d, n, N_MEM * MEM_ROWS, LANES)


def _cache_from_rows(c, n):
    c = c.reshape(DEPTH, n, N_MEM, HALF_HEAD, N_MEM_HEADS, LANES).transpose(0, 1, 2, 4, 3, 5)
    return c.reshape(DEPTH, n, N_MEM, N_MEM_HEADS, MEM_HEAD_DIM)


def kernel(x_prompt, x_sample, state_pool, state_conf, state_gconv, state_ffn, cache_mem_k, cache_mem_v, mem_prompt, norm_gains, w_in_even, w_pool, pool_scale, conf_w, conf_b, conf_ln_g, conf_ln_b, w_out_even, w_in_odd, gconv_w, w_out_odd, w_mem_q, w_mem_k, w_mem_v, w_mem_o, w_ffn_gate, w_ffn_up, ffn_conv_w, ffn_conv_b, w_ffn_down):
    prm = dict(
        gains=norm_gains.transpose(1, 0, 2),
        conf_w=conf_w.transpose(1, 0, 2),
        gconv_w=gconv_w.transpose(1, 0, 2),
        ffn_conv_w=ffn_conv_w.transpose(1, 0, 2),
        pool_scale=pool_scale, conf_b=conf_b, conf_ln_g=conf_ln_g, conf_ln_b=conf_ln_b, ffn_conv_b=ffn_conv_b,
        w_in_even=w_in_even.astype(BF16), w_pool=w_pool.astype(BF16), w_out_even=w_out_even.astype(BF16),
        w_in_odd=w_in_odd.astype(BF16), w_out_odd=w_out_odd.astype(BF16),
        w_mem_q=w_mem_q.astype(BF16), w_mem_o=w_mem_o.astype(BF16),
        w_ffn_gate=w_ffn_gate.astype(BF16), w_ffn_up=w_ffn_up.astype(BF16), w_ffn_down=w_ffn_down.astype(BF16))

    k32, v32, kb, vb = _mem_kv(mem_prompt.reshape(BATCH * N_MEM, D_MODEL), prm["gains"],
                               w_mem_k.astype(BF16), w_mem_v.astype(BF16), tile=PROMPT_TILE)
    kb = kb.reshape(DEPTH * BATCH, N_MEM, D_MODEL)
    vb = vb.reshape(DEPTH * BATCH, N_MEM, D_MODEL)

    def attend_prompt(x, l):
        return _attn_prompt(x, kb, vb, prm, l, ts=PROMPT_TILE)

    y_p, pool_p, conf_p, gconv_p, ffn_p = _trunk(
        x_prompt.reshape(BATCH * SEQ, D_MODEL), None, attend_prompt, prm,
        nseq=BATCH, bs=1, ts=PROMPT_TILE, pos0=0)

    states = dict(pool=state_pool.transpose(0, 2, 1, 3), conf=state_conf.transpose(0, 2, 1, 3),
                  gconv=state_gconv, ffn=state_ffn)
    ck, cv = _cache_rows(cache_mem_k), _cache_rows(cache_mem_v)

    def attend_sample(x, l):
        return _attn_sample(x, ck, cv, prm, l, tile=SAMPLE_SEQS * DEC_SEQ, bs=ATTN_SAMPLE_SEQS)

    y_s, pool_s, conf_s, gconv_s, ffn_s = _trunk(
        x_sample.reshape(DEC_BATCH * DEC_SEQ, D_MODEL), states, attend_sample, prm,
        nseq=DEC_BATCH, bs=SAMPLE_SEQS, ts=DEC_SEQ, pos0=PAST_LEN)

    return (y_p.reshape(BATCH, SEQ, D_MODEL), y_s.reshape(DEC_BATCH, DEC_SEQ, D_MODEL),
            pool_p, conf_p, gconv_p, ffn_p,
            _cache_from_rows(k32.reshape(DEPTH, BATCH, N_MEM * MEM_ROWS, LANES), BATCH),
            _cache_from_rows(v32.reshape(DEPTH, BATCH, N_MEM * MEM_ROWS, LANES), BATCH),
            pool_s, conf_s, gconv_s, ffn_s)
```

```python
import functools

import jax
import jax.numpy as jnp
from jax import lax
from jax.experimental import pallas as pl
from jax.experimental.pallas import tpu as pltpu

D_MODEL = 1024
BATCH = 8
SEQ = 2048
DEPTH = 4
DEC_BATCH = 128
DEC_SEQ = 8
PAST_LEN = 16384
N_EVEN = (DEPTH + 1) // 2
N_ODD = DEPTH // 2
POOL_WINDOWS = (2, 4, 8, 16)
D_POOL = 512
POOL_BUF = 15
D_CONF = 512
CONF_WIDTH = 31
D_GCONV = 1024
GCONV_WIDTH = 3
D_FF = 2816
FFN_CONV_WIDTH = 3
N_MEM = 256
N_MEM_HEADS = 4
MEM_HEAD_DIM = 256
RMS_EPS = 1e-6
LN_EPS = 1e-5
NORM_MIX_PRE, NORM_MIX_POST, NORM_X_PRE, NORM_X_POST, NORM_FFN_PRE, NORM_FFN_POST, NORM_MEM = range(7)

F32 = jnp.float32
BF16 = jnp.bfloat16

LANES = 128
SUBLANES = 8
VMEM_LIMIT_BYTES = 56 * 1024 * 1024

POOL_HIST = 16
CONF_HIST = 32
SHORT_HIST = 8

FFN_CHUNKS = ((0, 768), (768, 1536), (1536, 2304), (2304, 2816))

PROMPT_TILE = 512
SAMPLE_SEQS = 32
ATTN_SAMPLE_SEQS = 4
CONV_ROWS = 128
MIXER_SUB_TILES = 1

MEM_ROWS = N_MEM_HEADS * MEM_HEAD_DIM // LANES
HALF_HEAD = MEM_HEAD_DIM // LANES


def _rms(x, g):
    ms = jnp.mean(x * x, axis=-1, keepdims=True)
    return x * lax.rsqrt(ms + RMS_EPS) * g


def _mm(a, w):
    return jnp.dot(a.astype(BF16), w, preferred_element_type=F32)


def _lanes(i):
    return slice(i * LANES, (i + 1) * LANES)


def _seq_view(scr, bs):
    n_blk, total, _ = scr.shape
    return scr.reshape(n_blk, bs, total // bs, LANES)


def _carry_history(scr, bs, hist, ts, nj):
    if nj > 1:
        view = _seq_view(scr, bs)

        @pl.when(pl.program_id(1) > 0)
        def _():
            for g in range(scr.shape[0]):
                view[g, :, 0:hist, :] = view[g, :, ts:ts + hist, :]


def _first_tile(fn):
    pl.when(pl.program_id(1) == 0)(fn)


def _load_tm_state(scr, st_ref, bs, hist, n_state, ts):
    rows = hist + ts
    if st_ref is None:
        view = _seq_view(scr, bs)

        def fill():
            for g in range(scr.shape[0]):
                view[g, :, 0:hist, :] = jnp.zeros((bs, hist, LANES), F32)
    else:
        def fill():
            for g in range(scr.shape[0]):
                for r in range(n_state):
                    scr[g, pl.ds(hist - n_state + r, bs, stride=rows), :] = st_ref[r, :, _lanes(g)]
    _first_tile(fill)


def _store_tm_state(ns_ref, scr, bs, hist, n_state, ts):
    rows = hist + ts
    for g in range(scr.shape[0]):
        for r in range(n_state):
            ns_ref[r, :, _lanes(g)] = scr[g, pl.ds(ts + hist - n_state + r, bs, stride=rows), :]


def _store_tail(ns_ref, scr, bs, hist, ts):
    view = _seq_view(scr, bs)
    for g in range(scr.shape[0]):
        ns_ref[:, :, _lanes(g)] = view[g, :, ts:ts + hist, :]


def _load_bm_state(scr, st_ref, bs, hist, n_state):
    view = _seq_view(scr, bs)

    def fill():
        for g in range(scr.shape[0]):
            if st_ref is None:
                view[g, :, hist - n_state:hist, :] = jnp.zeros((bs, n_state, LANES), F32)
            else:
                view[g, :, hist - n_state:hist, :] = st_ref[:, :, _lanes(g)]
    _first_tile(fill)


def _sub_tiles(bs, ts, n_sub):
    if bs == 1:
        return [(0, 1, s * (ts // n_sub), ts // n_sub) for s in range(n_sub)]
    return [(s * (bs // n_sub), bs // n_sub, 0, ts) for s in range(n_sub)]


def _even_kernel(*refs, layer, bs, ts, nj, pos0, has_state, n_sub):
    if has_state:
        x_ref, sp_ref, sc_ref, *refs = refs
    else:
        x_ref, *refs = refs
        sp_ref = sc_ref = None
    (gains_ref, w_in_ref, w_pool_ref, pscale_ref, cw_ref, cb_ref, lng_ref, lnb_ref, w_out_ref,
     xo_ref, npool_ref, nconf_ref, a_scr, u_scr, cv_scr) = refs
    n_blk = D_POOL // LANES
    e = layer // 2
    _load_tm_state(a_scr, sp_ref, bs, POOL_HIST, POOL_BUF, ts)
    _load_tm_state(u_scr, sc_ref, bs, CONF_HIST, CONF_WIDTH - 1, ts)
    _carry_history(a_scr, bs, POOL_HIST, ts, nj)
    _carry_history(u_scr, bs, CONF_HIST, ts, nj)
    a_view = _seq_view(a_scr, bs)
    u_view = _seq_view(u_scr, bs)
    base = CONF_HIST - (CONF_WIDTH - 1)

    def project(sub):
        b0, nb, r0, nr = sub
        rows = slice(b0 * ts + r0, b0 * ts + r0 + nb * nr)
        h = _rms(x_ref[rows, :], gains_ref[NORM_MIX_PRE, layer:layer + 1, :])
        p = _mm(h, w_in_ref[...])
        u = p[:, D_POOL:D_POOL + D_CONF] * jax.nn.sigmoid(p[:, D_POOL + D_CONF:])
        for g in range(n_blk):
            a_view[g, b0:b0 + nb, POOL_HIST + r0:POOL_HIST + r0 + nr, :] = p[:, _lanes(g)].reshape(nb, nr, LANES)
            u_view[g, b0:b0 + nb, CONF_HIST + r0:CONF_HIST + r0 + nr, :] = u[:, _lanes(g)].reshape(nb, nr, LANES)

    def mix(sub):
        b0, nb, r0, nr = sub
        mm = nb * nr
        rows = slice(b0 * ts + r0, b0 * ts + r0 + mm)
        pos = pos0 + pl.program_id(1) * ts + r0 + lax.broadcasted_iota(jnp.int32, (1, nr, LANES), 1)
        ya = []
        for g, w in enumerate(POOL_WINDOWS):
            lo = POOL_HIST + r0
            cur = a_view[g, b0:b0 + nb, lo:lo + nr, :]
            win = cur
            for i in range(1, w):
                win = win + a_view[g, b0:b0 + nb, lo - i:lo - i + nr, :]
            inv_cnt = 1.0 / jnp.minimum(pos + 1, w).astype(F32)
            pooled = win * inv_cnt - cur
            ya.append(_mm(pooled.reshape(mm, LANES), w_pool_ref[g]))
        ya = jnp.concatenate(ya, axis=-1) * pscale_ref[e:e + 1, :]

        if nr >= CONV_ROWS:
            bb, rb = 1, CONV_ROWS
        else:
            bb, rb = CONV_ROWS // nr, nr
        cols = []
        for c in range(n_blk):
            blocks = []
            for bi in range(b0, b0 + nb, bb):
                for ri in range(r0, r0 + nr, rb):
                    acc = None
                    for k in range(CONF_WIDTH):
                        term = (u_view[c, bi:bi + bb, base + k + ri:base + k + ri + rb, :]
                                * cw_ref[k, e:e + 1, _lanes(c)].reshape(1, 1, LANES))
                        acc = term if acc is None else acc + term
                    blocks.append(acc.reshape(bb * rb, LANES))
            cols.append(jnp.concatenate(blocks, axis=0))
        cb = jnp.concatenate(cols, axis=-1) + cb_ref[e:e + 1, :]
        mu = jnp.mean(cb, axis=-1, keepdims=True)
        xc = cb - mu
        var = jnp.mean(xc * xc, axis=-1, keepdims=True)
        ln = xc * lax.rsqrt(var + LN_EPS) * lng_ref[e:e + 1, :] + lnb_ref[e:e + 1, :]
        yb = ln * jax.nn.sigmoid(ln)

        y = _mm(jnp.concatenate([ya, yb], axis=-1), w_out_ref[...])
        xo_ref[rows, :] = x_ref[rows, :] + _rms(y, gains_ref[NORM_MIX_POST, layer:layer + 1, :])

    prev = None
    for sub in _sub_tiles(bs, ts, n_sub):
        project(sub)
        if prev is not None:
            mix(prev)
        prev = sub
    mix(prev)

    if has_state:
        _store_tm_state(npool_ref, a_scr, bs, POOL_HIST, POOL_BUF, ts)
        _store_tm_state(nconf_ref, u_scr, bs, CONF_HIST, CONF_WIDTH - 1, ts)
    else:
        _store_tail(npool_ref, a_scr, bs, POOL_HIST, ts)
        _store_tail(nconf_ref, u_scr, bs, CONF_HIST, ts)


def _odd_kernel(*refs, layer, bs, ts, nj, has_state):
    if has_state:
        x_ref, sg_ref, *refs = refs
    else:
        x_ref, *refs = refs
        sg_ref = None
    gains_ref, w_in_ref, gw_ref, w_out_ref, xo_ref, ng_ref, v_scr = refs
    m = bs * ts
    n_blk = D_GCONV // LANES
    n_state = GCONV_WIDTH - 1
    o = layer // 2
    _load_bm_state(v_scr, sg_ref, bs, SHORT_HIST, n_state)
    _carry_history(v_scr, bs, SHORT_HIST, ts, nj)
    view = _seq_view(v_scr, bs)

    x = x_ref[...]
    h = _rms(x, gains_ref[NORM_MIX_PRE, layer:layer + 1, :])
    p = _mm(h, w_in_ref[...])
    v = p[:, 2 * D_GCONV:] * p[:, :D_GCONV]
    ys = []
    for g in range(n_blk):
        view[g, :, SHORT_HIST:SHORT_HIST + ts, :] = v[:, _lanes(g)].reshape(bs, ts, LANES)
        conv = None
        for k in range(GCONV_WIDTH):
            r = SHORT_HIST - n_state + k
            term = view[g, :, r:r + ts, :] * gw_ref[k, o:o + 1, _lanes(g)].reshape(1, 1, LANES)
            conv = term if conv is None else conv + term
        ys.append(conv.reshape(m, LANES))
    y = p[:, D_GCONV:2 * D_GCONV] * jnp.concatenate(ys, axis=-1)
    out = _mm(y, w_out_ref[...])
    xo_ref[...] = x + _rms(out, gains_ref[NORM_MIX_POST, layer:layer + 1, :])
    for g in range(n_blk):
        ng_ref[:, :, _lanes(g)] = view[g, :, ts + SHORT_HIST - n_state:ts + SHORT_HIST, :]


def _ffn_kernel(*refs, layer, bs, ts, nj, has_state):
    if has_state:
        x_ref, sf_ref, *refs = refs
    else:
        x_ref, *refs = refs
        sf_ref = None
    gains_ref, wg_ref, wu_ref, cw_ref, cb_ref, wd_ref, xo_ref, nf_ref, g_scr = refs
    m = bs * ts
    n_blk = D_FF // LANES
    n_state = FFN_CONV_WIDTH - 1
    _load_bm_state(g_scr, sf_ref, bs, SHORT_HIST, n_state)
    _carry_history(g_scr, bs, SHORT_HIST, ts, nj)
    view = _seq_view(g_scr, bs)

    x = x_ref[...]
    h = _rms(x, gains_ref[NORM_FFN_PRE, layer:layer + 1, :]).astype(BF16)
    acc = None
    for c0, c1 in FFN_CHUNKS:
        gate = jnp.dot(h, wg_ref[:, c0:c1], preferred_element_type=F32)
        up = jnp.dot(h, wu_ref[:, c0:c1], preferred_element_type=F32)
        gcs = []
        for l in range(c0 // LANES, c1 // LANES):
            off = l * LANES - c0
            view[l, :, SHORT_HIST:SHORT_HIST + ts, :] = gate[:, off:off + LANES].reshape(bs, ts, LANES)
            conv = None
            for k in range(FFN_CONV_WIDTH):
                r = SHORT_HIST - n_state + k
                term = view[l, :, r:r + ts, :] * cw_ref[k, layer:layer + 1, _lanes(l)].reshape(1, 1, LANES)
                conv = term if conv is None else conv + term
            gcs.append(conv.reshape(m, LANES))
        gc = jnp.concatenate(gcs, axis=-1) + cb_ref[layer:layer + 1, c0:c1]
        act = gc * jax.nn.sigmoid(gc) * up
        part = _mm(act, wd_ref[c0:c1, :])
        acc = part if acc is None else acc + part
    xo_ref[...] = x + _rms(acc, gains_ref[NORM_FFN_POST, layer:layer + 1, :])
    for l in range(n_blk):
        nf_ref[:, :, _lanes(l)] = view[l, :, ts + SHORT_HIST - n_state:ts + SHORT_HIST, :]


def _softmax_rows(s):
    e = jnp.exp(s - jnp.max(s, axis=-1, keepdims=True))
    return e * (1.0 / jnp.sum(e, axis=-1, keepdims=True))


def _attend(q, k_of_head, v_of_head):
    outs = []
    for hd in range(N_MEM_HEADS):
        cols = slice(hd * MEM_HEAD_DIM, (hd + 1) * MEM_HEAD_DIM)
        s = lax.dot_general(q[:, cols], k_of_head(hd), (((1,), (1,)), ((), ())),
                            preferred_element_type=F32)
        outs.append(_mm(_softmax_rows(s), v_of_head(hd)))
    return jnp.concatenate(outs, axis=-1)


ATTN_SCALE = MEM_HEAD_DIM ** -0.5


def _head_cols(hd):
    return slice(hd * MEM_HEAD_DIM, (hd + 1) * MEM_HEAD_DIM)


def _attn_prompt_kernel(x_ref, k_ref, v_ref, gains_ref, wq_ref, wo_ref, xo_ref, *, layer):
    x = x_ref[...]
    h = _rms(x, gains_ref[NORM_X_PRE, layer:layer + 1, :])
    q = (_mm(h, wq_ref[...]) * ATTN_SCALE).astype(BF16)
    o = _attend(q, lambda hd: k_ref[:, _head_cols(hd)], lambda hd: v_ref[:, _head_cols(hd)])
    y = _mm(o, wo_ref[...])
    xo_ref[...] = x + _rms(y, gains_ref[NORM_X_POST, layer:layer + 1, :])


def _kv_kernel(mem_ref, gains_ref, wk_ref, wv_ref, k32_ref, v32_ref, kb_ref, vb_ref):
    layer = pl.program_id(0)
    rows = mem_ref.shape[0]
    mn = _rms(mem_ref[...], gains_ref[NORM_MEM, pl.ds(layer, 1), :]).astype(BF16)
    for w_ref, o32_ref, ob_ref in ((wk_ref, k32_ref, kb_ref), (wv_ref, v32_ref, vb_ref)):
        kv = jnp.dot(mn, w_ref[...], preferred_element_type=F32)
        ob_ref[...] = kv.astype(BF16)
        for hd in range(N_MEM_HEADS):
            for half in range(HALF_HEAD):
                c0 = hd * MEM_HEAD_DIM + half * LANES
                o32_ref[pl.ds(half * N_MEM_HEADS + hd, rows, stride=MEM_ROWS), :] = kv[:, c0:c0 + LANES]


def _qproj_kernel(x_ref, gains_ref, wq_ref, q_ref, *, layer):
    h = _rms(x_ref[...], gains_ref[NORM_X_PRE, layer:layer + 1, :])
    q_ref[...] = _mm(h, wq_ref[...]) * ATTN_SCALE


def _cache_flat(c_ref, b):
    blocks = [c_ref[b, pl.ds(half * N_MEM_HEADS + hd, N_MEM, stride=MEM_ROWS), :]
              for hd in range(N_MEM_HEADS) for half in range(HALF_HEAD)]
    return jnp.concatenate(blocks, axis=-1).astype(BF16)


def _attn_core_kernel(q_ref, k_ref, v_ref, o_ref, *, bs, ts):
    zero = jnp.zeros((ts, MEM_HEAD_DIM), F32)

    def scores(b):
        q = q_ref[b * ts:(b + 1) * ts, :]
        q_heads = jnp.concatenate(
            [jnp.concatenate([q[:, _head_cols(hd)] if hd == blk else zero for hd in range(N_MEM_HEADS)], axis=-1)
             for blk in range(N_MEM_HEADS)], axis=0).astype(BF16)
        return lax.dot_general(q_heads, _cache_flat(k_ref, b), (((1,), (1,)), ((), ())),
                               preferred_element_type=F32)

    def finish(b, s):
        o_all = _mm(_softmax_rows(s), _cache_flat(v_ref, b))
        o_ref[b * ts:(b + 1) * ts, :] = jnp.concatenate(
            [o_all[hd * ts:(hd + 1) * ts, _head_cols(hd)] for hd in range(N_MEM_HEADS)], axis=-1)

    pending = None
    for b in range(bs):
        s = scores(b)
        if pending is not None:
            finish(*pending)
        pending = (b, s)
    finish(*pending)


def _oproj_kernel(x_ref, o_ref, gains_ref, wo_ref, xo_ref, *, layer):
    y = _mm(o_ref[...], wo_ref[...])
    xo_ref[...] = x_ref[...] + _rms(y, gains_ref[NORM_X_POST, layer:layer + 1, :])


def _params(n_axes):
    return pltpu.CompilerParams(dimension_semantics=("arbitrary",) * n_axes,
                                vmem_limit_bytes=VMEM_LIMIT_BYTES)


def _whole(arr):
    nd = arr.ndim
    return pl.BlockSpec(arr.shape, lambda *_: (0,) * nd)


def _layer_weight(arr, layer):
    nd = arr.ndim
    return pl.BlockSpec((None,) + arr.shape[1:], lambda *_: (layer,) + (0,) * (nd - 1),
                        pipeline_mode=pl.Buffered(1))


def _row_spec(m, nj, width):
    return pl.BlockSpec((m, width), lambda b, j: (b * nj + j, 0))


def _tm_state_in(idx, n_state, bs, width):
    return pl.BlockSpec((None, n_state, bs, width), lambda b, j: (idx, 0, b, 0))


def _tm_state_out(n_state, bs, width):
    return pl.BlockSpec((n_state, bs, width), lambda b, j: (0, b, 0))


def _bm_state_in(idx, n_state, bs, width):
    return pl.BlockSpec((None, bs, n_state, width), lambda b, j: (idx, b, 0, 0))


def _bm_state_out(n_state, bs, width):
    return pl.BlockSpec((bs, n_state, width), lambda b, j: (b, 0, 0))


def _even_mixer(x, states, prm, layer, *, nseq, bs, ts, pos0):
    nb, nj = nseq // bs, x.shape[0] // (nseq * ts)
    m = bs * ts
    e = layer // 2
    has_state = states is not None
    n_sub = MIXER_SUB_TILES if bs == 1 else 1
    kern = functools.partial(_even_kernel, layer=layer, bs=bs, ts=ts, nj=nj, pos0=pos0, has_state=has_state,
                             n_sub=n_sub)
    small = [prm["gains"], prm["pool_scale"], prm["conf_w"], prm["conf_b"], prm["conf_ln_g"], prm["conf_ln_b"]]
    args = [x]
    in_specs = [_row_spec(m, nj, D_MODEL)]
    if has_state:
        args += [states["pool"], states["conf"]]
        in_specs += [_tm_state_in(e, POOL_BUF, bs, D_POOL), _tm_state_in(e, CONF_WIDTH - 1, bs, D_CONF)]
    args += [small[0], prm["w_in_even"], prm["w_pool"], *small[1:], prm["w_out_even"]]
    in_specs += [_whole(small[0]), _layer_weight(prm["w_in_even"], e), _layer_weight(prm["w_pool"], e),
                 *[_whole(a) for a in small[1:]], _layer_weight(prm["w_out_even"], e)]
    if has_state:
        state_specs = [_tm_state_out(POOL_BUF, bs, D_POOL), _tm_state_out(CONF_WIDTH - 1, bs, D_CONF)]
        state_shapes = [(POOL_BUF, nseq, D_POOL), (CONF_WIDTH - 1, nseq, D_CONF)]
    else:
        state_specs = [_bm_state_out(POOL_HIST, bs, D_POOL), _bm_state_out(CONF_HIST, bs, D_CONF)]
        state_shapes = [(nseq, POOL_HIST, D_POOL), (nseq, CONF_HIST, D_CONF)]
    return pl.pallas_call(
        kern,
        grid=(nb, nj),
        in_specs=in_specs,
        out_specs=[_row_spec(m, nj, D_MODEL), *state_specs],
        out_shape=[jax.ShapeDtypeStruct(x.shape, F32), *[jax.ShapeDtypeStruct(s, F32) for s in state_shapes]],
        scratch_shapes=[pltpu.VMEM((D_POOL // LANES, bs * (POOL_HIST + ts), LANES), F32),
                        pltpu.VMEM((D_CONF // LANES, bs * (CONF_HIST + ts), LANES), F32),
                        pltpu.VMEM((bs, ts, D_CONF), F32)],
        compiler_params=_params(2),
        name="even_mixer",
    )(*args)


def _odd_mixer(x, states, prm, layer, *, nseq, bs, ts):
    nb, nj = nseq // bs, x.shape[0] // (nseq * ts)
    m = bs * ts
    o = layer // 2
    has_state = states is not None
    kern = functools.partial(_odd_kernel, layer=layer, bs=bs, ts=ts, nj=nj, has_state=has_state)
    args = [x]
    in_specs = [_row_spec(m, nj, D_MODEL)]
    if has_state:
        args.append(states["gconv"])
        in_specs.append(_bm_state_in(o, GCONV_WIDTH - 1, bs, D_GCONV))
    args += [prm["gains"], prm["w_in_odd"], prm["gconv_w"], prm["w_out_odd"]]
    in_specs += [_whole(prm["gains"]), _layer_weight(prm["w_in_odd"], o), _whole(prm["gconv_w"]),
                 _layer_weight(prm["w_out_odd"], o)]
    return pl.pallas_call(
        kern,
        grid=(nb, nj),
        in_specs=in_specs,
        out_specs=[_row_spec(m, nj, D_MODEL), _bm_state_out(GCONV_WIDTH - 1, bs, D_GCONV)],
        out_shape=[jax.ShapeDtypeStruct(x.shape, F32),
                   jax.ShapeDtypeStruct((nseq, GCONV_WIDTH - 1, D_GCONV), F32)],
        scratch_shapes=[pltpu.VMEM((D_GCONV // LANES, bs * (SHORT_HIST + ts), LANES), F32)],
        compiler_params=_params(2),
        name="odd_mixer",
    )(*args)


def _conv_ffn(x, states, prm, layer, *, nseq, bs, ts):
    nb, nj = nseq // bs, x.shape[0] // (nseq * ts)
    m = bs * ts
    has_state = states is not None
    kern = functools.partial(_ffn_kernel, layer=layer, bs=bs, ts=ts, nj=nj, has_state=has_state)
    args = [x]
    in_specs = [_row_spec(m, nj, D_MODEL)]
    if has_state:
        args.append(states["ffn"])
        in_specs.append(_bm_state_in(layer, FFN_CONV_WIDTH - 1, bs, D_FF))
    args += [prm["gains"], prm["w_ffn_gate"], prm["w_ffn_up"], prm["ffn_conv_w"], prm["ffn_conv_b"],
             prm["w_ffn_down"]]
    in_specs += [_whole(prm["gains"]), _layer_weight(prm["w_ffn_gate"], layer),
                 _layer_weight(prm["w_ffn_up"], layer), _whole(prm["ffn_conv_w"]), _whole(prm["ffn_conv_b"]),
                 _layer_weight(prm["w_ffn_down"], layer)]
    return pl.pallas_call(
        kern,
        grid=(nb, nj),
        in_specs=in_specs,
        out_specs=[_row_spec(m, nj, D_MODEL), _bm_state_out(FFN_CONV_WIDTH - 1, bs, D_FF)],
        out_shape=[jax.ShapeDtypeStruct(x.shape, F32),
                   jax.ShapeDtypeStruct((nseq, FFN_CONV_WIDTH - 1, D_FF), F32)],
        scratch_shapes=[pltpu.VMEM((D_FF // LANES, bs * (SHORT_HIST + ts), LANES), F32)],
        compiler_params=_params(2),
        name="conv_ffn",
    )(*args)


def _attn_prompt(x, kb, vb, prm, layer, *, ts):
    nj = SEQ // ts
    kv_spec = pl.BlockSpec((None, N_MEM, D_MODEL), lambda b, j: (layer * BATCH + b, 0, 0))
    return pl.pallas_call(
        functools.partial(_attn_prompt_kernel, layer=layer),
        grid=(BATCH, nj),
        in_specs=[_row_spec(ts, nj, D_MODEL), kv_spec, kv_spec, _whole(prm["gains"]),
                  _layer_weight(prm["w_mem_q"], layer), _layer_weight(prm["w_mem_o"], layer)],
        out_specs=_row_spec(ts, nj, D_MODEL),
        out_shape=jax.ShapeDtypeStruct(x.shape, F32),
        compiler_params=_params(2),
        name="attn_prompt",
    )(x, kb, vb, prm["gains"], prm["w_mem_q"], prm["w_mem_o"])


def _mem_kv(mem, gains, wk, wv, *, tile):
    rows = mem.shape[0]
    w_spec = pl.BlockSpec((None, D_MODEL, D_MODEL), lambda l, i: (l, 0, 0))
    o32 = pl.BlockSpec((None, tile * MEM_ROWS, LANES), lambda l, i: (l, i, 0))
    ob = pl.BlockSpec((None, tile, D_MODEL), lambda l, i: (l, i, 0))
    return pl.pallas_call(
        _kv_kernel,
        grid=(DEPTH, rows // tile),
        in_specs=[pl.BlockSpec((tile, D_MODEL), lambda l, i: (i, 0)), _whole(gains), w_spec, w_spec],
        out_specs=[o32, o32, ob, ob],
        out_shape=[jax.ShapeDtypeStruct((DEPTH, rows * MEM_ROWS, LANES), F32)] * 2
        + [jax.ShapeDtypeStruct((DEPTH, rows, D_MODEL), BF16)] * 2,
        compiler_params=_params(2),
        name="mem_kv",
    )(mem, gains, wk, wv)


def _attn_sample(x, ck, cv, prm, layer, *, tile, bs):
    rows = x.shape[0]
    gains = prm["gains"]
    row1 = pl.BlockSpec((tile, D_MODEL), lambda i: (i, 0))
    q = pl.pallas_call(
        functools.partial(_qproj_kernel, layer=layer),
        grid=(rows // tile,),
        in_specs=[row1, _whole(gains), _layer_weight(prm["w_mem_q"], layer)],
        out_specs=row1,
        out_shape=jax.ShapeDtypeStruct((rows, D_MODEL), F32),
        compiler_params=_params(1),
        name="attn_q",
    )(x, gains, prm["w_mem_q"])
    rowc = pl.BlockSpec((bs * DEC_SEQ, D_MODEL), lambda i: (i, 0))
    kv_spec = pl.BlockSpec((None, bs, N_MEM * MEM_ROWS, LANES), lambda i: (layer, i, 0, 0))
    o = pl.pallas_call(
        functools.partial(_attn_core_kernel, bs=bs, ts=DEC_SEQ),
        grid=(DEC_BATCH // bs,),
        in_specs=[rowc, kv_spec, kv_spec],
        out_specs=rowc,
        out_shape=jax.ShapeDtypeStruct((rows, D_MODEL), F32),
        compiler_params=_params(1),
        name="attn_core",
    )(q, ck, cv)
    return pl.pallas_call(
        functools.partial(_oproj_kernel, layer=layer),
        grid=(rows // tile,),
        in_specs=[row1, row1, _whole(gains), _layer_weight(prm["w_mem_o"], layer)],
        out_specs=row1,
        out_shape=jax.ShapeDtypeStruct((rows, D_MODEL), F32),
        compiler_params=_params(1),
        name="attn_o",
    )(x, o, gains, prm["w_mem_o"])


def _trunk(x, states, attend, prm, *, nseq, bs, ts, pos0):
    new_pool, new_conf, new_gconv, new_ffn = [], [], [], []
    for l in range(DEPTH):
        if l % 2 == 0:
            x, sp, sc = _even_mixer(x, states, prm, l, nseq=nseq, bs=bs, ts=ts, pos0=pos0)
            new_pool.append(sp)
            new_conf.append(sc)
        else:
            x, sg = _odd_mixer(x, states, prm, l, nseq=nseq, bs=bs, ts=ts)
            new_gconv.append(sg)
        x = attend(x, l)
        x, sf = _conv_ffn(x, states, prm, l, nseq=nseq, bs=bs, ts=ts)
        new_ffn.append(sf)
    new_pool, new_conf = jnp.stack(new_pool), jnp.stack(new_conf)
    if states is None:
        new_pool = new_pool[:, :, POOL_HIST - POOL_BUF:, :]
        new_conf = new_conf[:, :, CONF_HIST - (CONF_WIDTH - 1):, :]
    else:
        new_pool, new_conf = new_pool.transpose(0, 2, 1, 3), new_conf.transpose(0, 2, 1, 3)
    return x, new_pool, new_conf, jnp.stack(new_gconv), jnp.stack(new_ffn)


def _cache_rows(c):
    d, n = c.shape[0], c.shape[1]
    c = c.reshape(d, n, N_MEM, N_MEM_HEADS, HALF_HEAD, LANES).transpose(0, 1, 2, 4, 3, 5)
    return c.reshape(d, n, N_MEM * MEM_ROWS, LANES)


def _cache_from_rows(c, n):
    c = c.reshape(DEPTH, n, N_MEM, HALF_HEAD, N_MEM_HEADS, LANES).transpose(0, 1, 2, 4, 3, 5)
    return c.reshape(DEPTH, n, N_MEM, N_MEM_HEADS, MEM_HEAD_DIM)


def kernel(x_prompt, x_sample, state_pool, state_conf, state_gconv, state_ffn, cache_mem_k, cache_mem_v, mem_prompt, norm_gains, w_in_even, w_pool, pool_scale, conf_w, conf_b, conf_ln_g, conf_ln_b, w_out_even, w_in_odd, gconv_w, w_out_odd, w_mem_q, w_mem_k, w_mem_v, w_mem_o, w_ffn_gate, w_ffn_up, ffn_conv_w, ffn_conv_b, w_ffn_down):
    prm = dict(
        gains=norm_gains.transpose(1, 0, 2),
        conf_w=conf_w.transpose(1, 0, 2),
        gconv_w=gconv_w.transpose(1, 0, 2),
        ffn_conv_w=ffn_conv_w.transpose(1, 0, 2),
        pool_scale=pool_scale, conf_b=conf_b, conf_ln_g=conf_ln_g, conf_ln_b=conf_ln_b, ffn_conv_b=ffn_conv_b,
        w_in_even=w_in_even.astype(BF16), w_pool=w_pool.astype(BF16), w_out_even=w_out_even.astype(BF16),
        w_in_odd=w_in_odd.astype(BF16), w_out_odd=w_out_odd.astype(BF16),
        w_mem_q=w_mem_q.astype(BF16), w_mem_o=w_mem_o.astype(BF16),
        w_ffn_gate=w_ffn_gate.astype(BF16), w_ffn_up=w_ffn_up.astype(BF16), w_ffn_down=w_ffn_down.astype(BF16))

    k32, v32, kb, vb = _mem_kv(mem_prompt.reshape(BATCH * N_MEM, D_MODEL), prm["gains"],
                               w_mem_k.astype(BF16), w_mem_v.astype(BF16), tile=PROMPT_TILE)
    kb = kb.reshape(DEPTH * BATCH, N_MEM, D_MODEL)
    vb = vb.reshape(DEPTH * BATCH, N_MEM, D_MODEL)

    def attend_prompt(x, l):
        return _attn_prompt(x, kb, vb, prm, l, ts=PROMPT_TILE)

    y_p, pool_p, conf_p, gconv_p, ffn_p = _trunk(
        x_prompt.reshape(BATCH * SEQ, D_MODEL), None, attend_prompt, prm,
        nseq=BATCH, bs=1, ts=PROMPT_TILE, pos0=0)

    states = dict(pool=state_pool.transpose(0, 2, 1, 3), conf=state_conf.transpose(0, 2, 1, 3),
                  gconv=state_gconv, ffn=state_ffn)
    ck, cv = _cache_rows(cache_mem_k), _cache_rows(cache_mem_v)

    def attend_sample(x, l):
        return _attn_sample(x, ck, cv, prm, l, tile=SAMPLE_SEQS * DEC_SEQ, bs=ATTN_SAMPLE_SEQS)

    y_s, pool_s, conf_s, gconv_s, ffn_s = _trunk(
        x_sample.reshape(DEC_BATCH * DEC_SEQ, D_MODEL), states, attend_sample, prm,
        nseq=DEC_BATCH, bs=SAMPLE_SEQS, ts=DEC_SEQ, pos0=PAST_LEN)

    return (y_p.reshape(BATCH, SEQ, D_MODEL), y_s.reshape(DEC_BATCH, DEC_SEQ, D_MODEL),
            pool_p, conf_p, gconv_p, ffn_p,
            _cache_from_rows(k32.reshape(DEPTH, BATCH, N_MEM * MEM_ROWS, LANES), BATCH),
            _cache_from_rows(v32.reshape(DEPTH, BATCH, N_MEM * MEM_ROWS, LANES), BATCH),
            pool_s, conf_s, gconv_s, ffn_s)
```

```python
import functools

import jax
import jax.numpy as jnp
from jax import lax
from jax.experimental import pallas as pl
from jax.experimental.pallas import tpu as pltpu

D_MODEL = 1024
BATCH = 8
SEQ = 2048
DEPTH = 4
DEC_BATCH = 128
DEC_SEQ = 8
PAST_LEN = 16384
N_EVEN = (DEPTH + 1) // 2
N_ODD = DEPTH // 2
POOL_WINDOWS = (2, 4, 8, 16)
D_POOL = 512
POOL_BUF = 15
D_CONF = 512
CONF_WIDTH = 31
D_GCONV = 1024
GCONV_WIDTH = 3
D_FF = 2816
FFN_CONV_WIDTH = 3
N_MEM = 256
N_MEM_HEADS = 4
MEM_HEAD_DIM = 256
RMS_EPS = 1e-6
LN_EPS = 1e-5
NORM_MIX_PRE, NORM_MIX_POST, NORM_X_PRE, NORM_X_POST, NORM_FFN_PRE, NORM_FFN_POST, NORM_MEM = range(7)

F32 = jnp.float32
BF16 = jnp.bfloat16

LANES = 128
SUBLANES = 8
VMEM_LIMIT_BYTES = 56 * 1024 * 1024

POOL_HIST = 16
CONF_HIST = 32
SHORT_HIST = 8

FFN_CHUNKS = ((0, 1024), (1024, 2048), (2048, 2816))

PROMPT_TILE = 512
SAMPLE_SEQS = 32
CONV_ROWS = 128
MIXER_SUB_TILES = 1

MEM_ROWS = N_MEM_HEADS * MEM_HEAD_DIM // LANES
HALF_HEAD = MEM_HEAD_DIM // LANES


def _rms(x, g):
    ms = jnp.mean(x * x, axis=-1, keepdims=True)
    return x * lax.rsqrt(ms + RMS_EPS) * g


def _mm(a, w):
    return jnp.dot(a.astype(BF16), w, preferred_element_type=F32)


def _lanes(i):
    return slice(i * LANES, (i + 1) * LANES)


def _seq_view(scr, bs):
    n_blk, total, _ = scr.shape
    return scr.reshape(n_blk, bs, total // bs, LANES)


def _carry_history(scr, bs, hist, ts, nj):
    if nj > 1:
        view = _seq_view(scr, bs)

        @pl.when(pl.program_id(1) > 0)
        def _():
            for g in range(scr.shape[0]):
                view[g, :, 0:hist, :] = view[g, :, ts:ts + hist, :]


def _first_tile(fn):
    pl.when(pl.program_id(1) == 0)(fn)


def _load_tm_state(scr, st_ref, bs, hist, n_state, ts):
    rows = hist + ts
    if st_ref is None:
        view = _seq_view(scr, bs)

        def fill():
            for g in range(scr.shape[0]):
                view[g, :, 0:hist, :] = jnp.zeros((bs, hist, LANES), F32)
    else:
        def fill():
            for g in range(scr.shape[0]):
                for r in range(n_state):
                    scr[g, pl.ds(hist - n_state + r, bs, stride=rows), :] = st_ref[r, :, _lanes(g)]
    _first_tile(fill)


def _store_tm_state(ns_ref, scr, bs, hist, n_state, ts):
    rows = hist + ts
    for g in range(scr.shape[0]):
        for r in range(n_state):
            ns_ref[r, :, _lanes(g)] = scr[g, pl.ds(ts + hist - n_state + r, bs, stride=rows), :]


def _store_tail(ns_ref, scr, bs, hist, ts):
    view = _seq_view(scr, bs)
    for g in range(scr.shape[0]):
        ns_ref[:, :, _lanes(g)] = view[g, :, ts:ts + hist, :]


def _load_bm_state(scr, st_ref, bs, hist, n_state):
    view = _seq_view(scr, bs)

    def fill():
        for g in range(scr.shape[0]):
            if st_ref is None:
                view[g, :, hist - n_state:hist, :] = jnp.zeros((bs, n_state, LANES), F32)
            else:
                view[g, :, hist - n_state:hist, :] = st_ref[:, :, _lanes(g)]
    _first_tile(fill)


def _sub_tiles(bs, ts, n_sub):
    if bs == 1:
        return [(0, 1, s * (ts // n_sub), ts // n_sub) for s in range(n_sub)]
    return [(s * (bs // n_sub), bs // n_sub, 0, ts) for s in range(n_sub)]


def _even_kernel(*refs, layer, bs, ts, nj, pos0, has_state, n_sub):
    if has_state:
        x_ref, sp_ref, sc_ref, *refs = refs
    else:
        x_ref, *refs = refs
        sp_ref = sc_ref = None
    (gains_ref, w_in_ref, w_pool_ref, pscale_ref, cw_ref, cb_ref, lng_ref, lnb_ref, w_out_ref,
     xo_ref, npool_ref, nconf_ref, a_scr, u_scr) = refs
    n_blk = D_POOL // LANES
    e = layer // 2
    _load_tm_state(a_scr, sp_ref, bs, POOL_HIST, POOL_BUF, ts)
    _load_tm_state(u_scr, sc_ref, bs, CONF_HIST, CONF_WIDTH - 1, ts)
    _carry_history(a_scr, bs, POOL_HIST, ts, nj)
    _carry_history(u_scr, bs, CONF_HIST, ts, nj)
    a_view = _seq_view(a_scr, bs)
    u_view = _seq_view(u_scr, bs)
    base = CONF_HIST - (CONF_WIDTH - 1)

    def project(sub):
        b0, nb, r0, nr = sub
        rows = slice(b0 * ts + r0, b0 * ts + r0 + nb * nr)
        h = _rms(x_ref[rows, :], gains_ref[NORM_MIX_PRE, layer:layer + 1, :])
        p = _mm(h, w_in_ref[...])
        u = p[:, D_POOL:D_POOL + D_CONF] * jax.nn.sigmoid(p[:, D_POOL + D_CONF:])
        for g in range(n_blk):
            a_view[g, b0:b0 + nb, POOL_HIST + r0:POOL_HIST + r0 + nr, :] = p[:, _lanes(g)].reshape(nb, nr, LANES)
            u_view[g, b0:b0 + nb, CONF_HIST + r0:CONF_HIST + r0 + nr, :] = u[:, _lanes(g)].reshape(nb, nr, LANES)

    def mix(sub):
        b0, nb, r0, nr = sub
        mm = nb * nr
        rows = slice(b0 * ts + r0, b0 * ts + r0 + mm)
        pos = pos0 + pl.program_id(1) * ts + r0 + lax.broadcasted_iota(jnp.int32, (1, nr, LANES), 1)
        ya = []
        for g, w in enumerate(POOL_WINDOWS):
            lo = POOL_HIST + r0
            cur = a_view[g, b0:b0 + nb, lo:lo + nr, :]
            win = cur
            for i in range(1, w):
                win = win + a_view[g, b0:b0 + nb, lo - i:lo - i + nr, :]
            inv_cnt = 1.0 / jnp.minimum(pos + 1, w).astype(F32)
            pooled = win * inv_cnt - cur
            ya.append(_mm(pooled.reshape(mm, LANES), w_pool_ref[g]))
        ya = jnp.concatenate(ya, axis=-1) * pscale_ref[e:e + 1, :]

        if nr >= CONV_ROWS:
            bb, rb = 1, CONV_ROWS
        else:
            bb, rb = CONV_ROWS // nr, nr
        cols = []
        for c in range(n_blk):
            blocks = []
            for bi in range(b0, b0 + nb, bb):
                for ri in range(r0, r0 + nr, rb):
                    acc = None
                    for k in range(CONF_WIDTH):
                        term = (u_view[c, bi:bi + bb, base + k + ri:base + k + ri + rb, :]
                                * cw_ref[k, e:e + 1, _lanes(c)].reshape(1, 1, LANES))
                        acc = term if acc is None else acc + term
                    blocks.append(acc.reshape(bb * rb, LANES))
            cols.append(jnp.concatenate(blocks, axis=0))
        cb = jnp.concatenate(cols, axis=-1) + cb_ref[e:e + 1, :]
        mu = jnp.mean(cb, axis=-1, keepdims=True)
        xc = cb - mu
        var = jnp.mean(xc * xc, axis=-1, keepdims=True)
        ln = xc * lax.rsqrt(var + LN_EPS) * lng_ref[e:e + 1, :] + lnb_ref[e:e + 1, :]
        yb = ln * jax.nn.sigmoid(ln)

        y = _mm(jnp.concatenate([ya, yb], axis=-1), w_out_ref[...])
        xo_ref[rows, :] = x_ref[rows, :] + _rms(y, gains_ref[NORM_MIX_POST, layer:layer + 1, :])

    prev = None
    for sub in _sub_tiles(bs, ts, n_sub):
        project(sub)
        if prev is not None:
            mix(prev)
        prev = sub
    mix(prev)

    if has_state:
        _store_tm_state(npool_ref, a_scr, bs, POOL_HIST, POOL_BUF, ts)
        _store_tm_state(nconf_ref, u_scr, bs, CONF_HIST, CONF_WIDTH - 1, ts)
    else:
        _store_tail(npool_ref, a_scr, bs, POOL_HIST, ts)
        _store_tail(nconf_ref, u_scr, bs, CONF_HIST, ts)


def _odd_kernel(*refs, layer, bs, ts, nj, has_state):
    if has_state:
        x_ref, sg_ref, *refs = refs
    else:
        x_ref, *refs = refs
        sg_ref = None
    gains_ref, w_in_ref, gw_ref, w_out_ref, xo_ref, ng_ref, v_scr = refs
    m = bs * ts
    n_blk = D_GCONV // LANES
    n_state = GCONV_WIDTH - 1
    o = layer // 2
    _load_bm_state(v_scr, sg_ref, bs, SHORT_HIST, n_state)
    _carry_history(v_scr, bs, SHORT_HIST, ts, nj)
    view = _seq_view(v_scr, bs)

    x = x_ref[...]
    h = _rms(x, gains_ref[NORM_MIX_PRE, layer:layer + 1, :])
    p = _mm(h, w_in_ref[...])
    v = p[:, 2 * D_GCONV:] * p[:, :D_GCONV]
    ys = []
    for g in range(n_blk):
        view[g, :, SHORT_HIST:SHORT_HIST + ts, :] = v[:, _lanes(g)].reshape(bs, ts, LANES)
        conv = None
        for k in range(GCONV_WIDTH):
            r = SHORT_HIST - n_state + k
            term = view[g, :, r:r + ts, :] * gw_ref[k, o:o + 1, _lanes(g)].reshape(1, 1, LANES)
            conv = term if conv is None else conv + term
        ys.append(conv.reshape(m, LANES))
    y = p[:, D_GCONV:2 * D_GCONV] * jnp.concatenate(ys, axis=-1)
    out = _mm(y, w_out_ref[...])
    xo_ref[...] = x + _rms(out, gains_ref[NORM_MIX_POST, layer:layer + 1, :])
    for g in range(n_blk):
        ng_ref[:, :, _lanes(g)] = view[g, :, ts + SHORT_HIST - n_state:ts + SHORT_HIST, :]


def _ffn_kernel(*refs, layer, bs, ts, nj, has_state):
    if has_state:
        x_ref, sf_ref, *refs = refs
    else:
        x_ref, *refs = refs
        sf_ref = None
    gains_ref, wg_ref, wu_ref, cw_ref, cb_ref, wd_ref, xo_ref, nf_ref, g_scr = refs
    m = bs * ts
    n_blk = D_FF // LANES
    n_state = FFN_CONV_WIDTH - 1
    _load_bm_state(g_scr, sf_ref, bs, SHORT_HIST, n_state)
    _carry_history(g_scr, bs, SHORT_HIST, ts, nj)
    view = _seq_view(g_scr, bs)

    x = x_ref[...]
    h = _rms(x, gains_ref[NORM_FFN_PRE, layer:layer + 1, :]).astype(BF16)
    acc = None
    for c0, c1 in FFN_CHUNKS:
        gate = jnp.dot(h, wg_ref[:, c0:c1], preferred_element_type=F32)
        up = jnp.dot(h, wu_ref[:, c0:c1], preferred_element_type=F32)
        gcs = []
        for l in range(c0 // LANES, c1 // LANES):
            off = l * LANES - c0
            view[l, :, SHORT_HIST:SHORT_HIST + ts, :] = gate[:, off:off + LANES].reshape(bs, ts, LANES)
            conv = None
            for k in range(FFN_CONV_WIDTH):
                r = SHORT_HIST - n_state + k
                term = view[l, :, r:r + ts, :] * cw_ref[k, layer:layer + 1, _lanes(l)].reshape(1, 1, LANES)
                conv = term if conv is None else conv + term
            gcs.append(conv.reshape(m, LANES))
        gc = jnp.concatenate(gcs, axis=-1) + cb_ref[layer:layer + 1, c0:c1]
        act = gc * jax.nn.sigmoid(gc) * up
        part = _mm(act, wd_ref[c0:c1, :])
        acc = part if acc is None else acc + part
    xo_ref[...] = x + _rms(acc, gains_ref[NORM_FFN_POST, layer:layer + 1, :])
    for l in range(n_blk):
        nf_ref[:, :, _lanes(l)] = view[l, :, ts + SHORT_HIST - n_state:ts + SHORT_HIST, :]


def _softmax_rows(s):
    e = jnp.exp(s - jnp.max(s, axis=-1, keepdims=True))
    return e * (1.0 / jnp.sum(e, axis=-1, keepdims=True))


ATTN_SCALE = MEM_HEAD_DIM ** -0.5


def _head_cols(hd):
    return slice(hd * MEM_HEAD_DIM, (hd + 1) * MEM_HEAD_DIM)


def _kv_kernel(mem_ref, gains_ref, wk_ref, wv_ref, k32_ref, v32_ref, kb_ref, vb_ref):
    layer = pl.program_id(0)
    rows = mem_ref.shape[0]
    mn = _rms(mem_ref[...], gains_ref[NORM_MEM, pl.ds(layer, 1), :]).astype(BF16)
    for w_ref, o32_ref, ob_ref in ((wk_ref, k32_ref, kb_ref), (wv_ref, v32_ref, vb_ref)):
        kv = jnp.dot(mn, w_ref[...], preferred_element_type=F32)
        ob_ref[...] = kv.astype(BF16)
        for hd in range(N_MEM_HEADS):
            for half in range(HALF_HEAD):
                c0 = hd * MEM_HEAD_DIM + half * LANES
                o32_ref[pl.ds(half * N_MEM_HEADS + hd, rows, stride=MEM_ROWS), :] = kv[:, c0:c0 + LANES]


def _qproj_kernel(x_ref, gains_ref, wq_ref, q_ref, *, layer):
    h = _rms(x_ref[...], gains_ref[NORM_X_PRE, layer:layer + 1, :])
    q_ref[...] = _mm(h, wq_ref[...]) * ATTN_SCALE


def _cache_flat(c_ref, b):
    blocks = [c_ref[b, pl.ds(half * N_MEM_HEADS + hd, N_MEM, stride=MEM_ROWS), :]
              for hd in range(N_MEM_HEADS) for half in range(HALF_HEAD)]
    return jnp.concatenate(blocks, axis=-1).astype(BF16)


def _cached_attention(q_ref, k_ref, v_ref, o_ref, ts):
    zero = jnp.zeros((ts, MEM_HEAD_DIM), F32)

    def scores(b):
        q = q_ref[b * ts:(b + 1) * ts, :]
        q_heads = jnp.concatenate(
            [jnp.concatenate([q[:, _head_cols(hd)] if hd == blk else zero for hd in range(N_MEM_HEADS)], axis=-1)
             for blk in range(N_MEM_HEADS)], axis=0).astype(BF16)
        return lax.dot_general(q_heads, _cache_flat(k_ref, b), (((1,), (1,)), ((), ())),
                               preferred_element_type=F32)

    def finish(b, s):
        o_all = _mm(_softmax_rows(s), _cache_flat(v_ref, b))
        o_ref[b * ts:(b + 1) * ts, :] = jnp.concatenate(
            [o_all[hd * ts:(hd + 1) * ts, _head_cols(hd)] for hd in range(N_MEM_HEADS)], axis=-1)

    return scores, finish


def _attn_fused_kernel(x_ref, k_ref, v_ref, gains_ref, wq_ref, wo_ref, qs_ref, ck_ref, cv_ref, xo_ref, os_ref,
                       *, layer, bs, ts):
    scores, finish = _cached_attention(qs_ref, ck_ref, cv_ref, os_ref, ts)
    x = x_ref[...]
    h = _rms(x, gains_ref[NORM_X_PRE, layer:layer + 1, :])
    q = (_mm(h, wq_ref[...]) * ATTN_SCALE).astype(BF16)
    outs = []
    pending = None
    for step in range(max(N_MEM_HEADS, bs)):
        s_sample = scores(step) if step < bs else None
        if step < N_MEM_HEADS:
            cols = _head_cols(step)
            s = lax.dot_general(q[:, cols], k_ref[:, cols], (((1,), (1,)), ((), ())), preferred_element_type=F32)
            outs.append(_mm(_softmax_rows(s), v_ref[:, cols]))
        if pending is not None:
            finish(*pending)
        pending = (step, s_sample) if s_sample is not None else None
    if pending is not None:
        finish(*pending)
    y = _mm(jnp.concatenate(outs, axis=-1), wo_ref[...])
    xo_ref[...] = x + _rms(y, gains_ref[NORM_X_POST, layer:layer + 1, :])


def _oproj_kernel(x_ref, o_ref, gains_ref, wo_ref, xo_ref, *, layer):
    y = _mm(o_ref[...], wo_ref[...])
    xo_ref[...] = x_ref[...] + _rms(y, gains_ref[NORM_X_POST, layer:layer + 1, :])


def _params(n_axes):
    return pltpu.CompilerParams(dimension_semantics=("arbitrary",) * n_axes,
                                vmem_limit_bytes=VMEM_LIMIT_BYTES)


def _whole(arr):
    nd = arr.ndim
    return pl.BlockSpec(arr.shape, lambda *_: (0,) * nd)


def _layer_weight(arr, layer):
    nd = arr.ndim
    return pl.BlockSpec((None,) + arr.shape[1:], lambda *_: (layer,) + (0,) * (nd - 1),
                        pipeline_mode=pl.Buffered(1))


def _row_spec(m, nj, width):
    return pl.BlockSpec((m, width), lambda b, j: (b * nj + j, 0))


def _tm_state_in(idx, n_state, bs, width):
    return pl.BlockSpec((None, n_state, bs, width), lambda b, j: (idx, 0, b, 0))


def _tm_state_out(n_state, bs, width):
    return pl.BlockSpec((n_state, bs, width), lambda b, j: (0, b, 0))


def _bm_state_in(idx, n_state, bs, width):
    return pl.BlockSpec((None, bs, n_state, width), lambda b, j: (idx, b, 0, 0))


def _bm_state_out(n_state, bs, width):
    return pl.BlockSpec((bs, n_state, width), lambda b, j: (b, 0, 0))


def _even_mixer(x, states, prm, layer, *, nseq, bs, ts, pos0):
    nb, nj = nseq // bs, x.shape[0] // (nseq * ts)
    m = bs * ts
    e = layer // 2
    has_state = states is not None
    n_sub = MIXER_SUB_TILES if bs == 1 else 1
    kern = functools.partial(_even_kernel, layer=layer, bs=bs, ts=ts, nj=nj, pos0=pos0, has_state=has_state,
                             n_sub=n_sub)
    small = [prm["gains"], prm["pool_scale"], prm["conf_w"], prm["conf_b"], prm["conf_ln_g"], prm["conf_ln_b"]]
    args = [x]
    in_specs = [_row_spec(m, nj, D_MODEL)]
    if has_state:
        args += [states["pool"], states["conf"]]
        in_specs += [_tm_state_in(e, POOL_BUF, bs, D_POOL), _tm_state_in(e, CONF_WIDTH - 1, bs, D_CONF)]
    args += [small[0], prm["w_in_even"], prm["w_pool"], *small[1:], prm["w_out_even"]]
    in_specs += [_whole(small[0]), _layer_weight(prm["w_in_even"], e), _layer_weight(prm["w_pool"], e),
                 *[_whole(a) for a in small[1:]], _layer_weight(prm["w_out_even"], e)]
    if has_state:
        state_specs = [_tm_state_out(POOL_BUF, bs, D_POOL), _tm_state_out(CONF_WIDTH - 1, bs, D_CONF)]
        state_shapes = [(POOL_BUF, nseq, D_POOL), (CONF_WIDTH - 1, nseq, D_CONF)]
    else:
        state_specs = [_bm_state_out(POOL_HIST, bs, D_POOL), _bm_state_out(CONF_HIST, bs, D_CONF)]
        state_shapes = [(nseq, POOL_HIST, D_POOL), (nseq, CONF_HIST, D_CONF)]
    return pl.pallas_call(
        kern,
        grid=(nb, nj),
        in_specs=in_specs,
        out_specs=[_row_spec(m, nj, D_MODEL), *state_specs],
        out_shape=[jax.ShapeDtypeStruct(x.shape, F32), *[jax.ShapeDtypeStruct(s, F32) for s in state_shapes]],
        scratch_shapes=[pltpu.VMEM((D_POOL // LANES, bs * (POOL_HIST + ts), LANES), F32),
                        pltpu.VMEM((D_CONF // LANES, bs * (CONF_HIST + ts), LANES), F32)],
        compiler_params=_params(2),
        name="even_mixer",
    )(*args)


def _odd_mixer(x, states, prm, layer, *, nseq, bs, ts):
    nb, nj = nseq // bs, x.shape[0] // (nseq * ts)
    m = bs * ts
    o = layer // 2
    has_state = states is not None
    kern = functools.partial(_odd_kernel, layer=layer, bs=bs, ts=ts, nj=nj, has_state=has_state)
    args = [x]
    in_specs = [_row_spec(m, nj, D_MODEL)]
    if has_state:
        args.append(states["gconv"])
        in_specs.append(_bm_state_in(o, GCONV_WIDTH - 1, bs, D_GCONV))
    args += [prm["gains"], prm["w_in_odd"], prm["gconv_w"], prm["w_out_odd"]]
    in_specs += [_whole(prm["gains"]), _layer_weight(prm["w_in_odd"], o), _whole(prm["gconv_w"]),
                 _layer_weight(prm["w_out_odd"], o)]
    return pl.pallas_call(
        kern,
        grid=(nb, nj),
        in_specs=in_specs,
        out_specs=[_row_spec(m, nj, D_MODEL), _bm_state_out(GCONV_WIDTH - 1, bs, D_GCONV)],
        out_shape=[jax.ShapeDtypeStruct(x.shape, F32),
                   jax.ShapeDtypeStruct((nseq, GCONV_WIDTH - 1, D_GCONV), F32)],
        scratch_shapes=[pltpu.VMEM((D_GCONV // LANES, bs * (SHORT_HIST + ts), LANES), F32)],
        compiler_params=_params(2),
        name="odd_mixer",
    )(*args)


def _conv_ffn(x, states, prm, layer, *, nseq, bs, ts):
    nb, nj = nseq // bs, x.shape[0] // (nseq * ts)
    m = bs * ts
    has_state = states is not None
    kern = functools.partial(_ffn_kernel, layer=layer, bs=bs, ts=ts, nj=nj, has_state=has_state)
    args = [x]
    in_specs = [_row_spec(m, nj, D_MODEL)]
    if has_state:
        args.append(states["ffn"])
        in_specs.append(_bm_state_in(layer, FFN_CONV_WIDTH - 1, bs, D_FF))
    args += [prm["gains"], prm["w_ffn_gate"], prm["w_ffn_up"], prm["ffn_conv_w"], prm["ffn_conv_b"],
             prm["w_ffn_down"]]
    in_specs += [_whole(prm["gains"]), _layer_weight(prm["w_ffn_gate"], layer),
                 _layer_weight(prm["w_ffn_up"], layer), _whole(prm["ffn_conv_w"]), _whole(prm["ffn_conv_b"]),
                 _layer_weight(prm["w_ffn_down"], layer)]
    return pl.pallas_call(
        kern,
        grid=(nb, nj),
        in_specs=in_specs,
        out_specs=[_row_spec(m, nj, D_MODEL), _bm_state_out(FFN_CONV_WIDTH - 1, bs, D_FF)],
        out_shape=[jax.ShapeDtypeStruct(x.shape, F32),
                   jax.ShapeDtypeStruct((nseq, FFN_CONV_WIDTH - 1, D_FF), F32)],
        scratch_shapes=[pltpu.VMEM((D_FF // LANES, bs * (SHORT_HIST + ts), LANES), F32)],
        compiler_params=_params(2),
        name="conv_ffn",
    )(*args)


def _attn_fused(x, kb, vb, q_s, ck, cv, prm, layer, *, ts):
    nj = SEQ // ts
    bs = DEC_BATCH // (BATCH * nj)
    kv_spec = pl.BlockSpec((None, N_MEM, D_MODEL), lambda b, j: (layer * BATCH + b, 0, 0))
    sample_rows = _row_spec(bs * DEC_SEQ, nj, D_MODEL)
    cache_spec = pl.BlockSpec((None, bs, N_MEM * MEM_ROWS, LANES), lambda b, j: (layer, b * nj + j, 0, 0))
    return pl.pallas_call(
        functools.partial(_attn_fused_kernel, layer=layer, bs=bs, ts=DEC_SEQ),
        grid=(BATCH, nj),
        in_specs=[_row_spec(ts, nj, D_MODEL), kv_spec, kv_spec, _whole(prm["gains"]),
                  _layer_weight(prm["w_mem_q"], layer), _layer_weight(prm["w_mem_o"], layer),
                  sample_rows, cache_spec, cache_spec],
        out_specs=[_row_spec(ts, nj, D_MODEL), sample_rows],
        out_shape=[jax.ShapeDtypeStruct(x.shape, F32), jax.ShapeDtypeStruct(q_s.shape, F32)],
        compiler_params=_params(2),
        name="attn_fused",
    )(x, kb, vb, prm["gains"], prm["w_mem_q"], prm["w_mem_o"], q_s, ck, cv)


def _mem_kv(mem, gains, wk, wv, *, tile):
    rows = mem.shape[0]
    w_spec = pl.BlockSpec((None, D_MODEL, D_MODEL), lambda l, i: (l, 0, 0))
    o32 = pl.BlockSpec((None, tile * MEM_ROWS, LANES), lambda l, i: (l, i, 0))
    ob = pl.BlockSpec((None, tile, D_MODEL), lambda l, i: (l, i, 0))
    return pl.pallas_call(
        _kv_kernel,
        grid=(DEPTH, rows // tile),
        in_specs=[pl.BlockSpec((tile, D_MODEL), lambda l, i: (i, 0)), _whole(gains), w_spec, w_spec],
        out_specs=[o32, o32, ob, ob],
        out_shape=[jax.ShapeDtypeStruct((DEPTH, rows * MEM_ROWS, LANES), F32)] * 2
        + [jax.ShapeDtypeStruct((DEPTH, rows, D_MODEL), BF16)] * 2,
        compiler_params=_params(2),
        name="mem_kv",
    )(mem, gains, wk, wv)


def _sample_qproj(x, prm, layer, *, tile):
    row = pl.BlockSpec((tile, D_MODEL), lambda i: (i, 0))
    return pl.pallas_call(
        functools.partial(_qproj_kernel, layer=layer),
        grid=(x.shape[0] // tile,),
        in_specs=[row, _whole(prm["gains"]), _layer_weight(prm["w_mem_q"], layer)],
        out_specs=row,
        out_shape=jax.ShapeDtypeStruct(x.shape, F32),
        compiler_params=_params(1),
        name="attn_q",
    )(x, prm["gains"], prm["w_mem_q"])


def _sample_oproj(x, o, prm, layer, *, tile):
    row = pl.BlockSpec((tile, D_MODEL), lambda i: (i, 0))
    return pl.pallas_call(
        functools.partial(_oproj_kernel, layer=layer),
        grid=(x.shape[0] // tile,),
        in_specs=[row, row, _whole(prm["gains"]), _layer_weight(prm["w_mem_o"], layer)],
        out_specs=row,
        out_shape=jax.ShapeDtypeStruct(x.shape, F32),
        compiler_params=_params(1),
        name="attn_o",
    )(x, o, prm["gains"], prm["w_mem_o"])


class _Group:
    def __init__(self, x, states, *, nseq, bs, ts, pos0):
        self.x, self.states = x, states
        self.tiling = dict(nseq=nseq, bs=bs, ts=ts)
        self.pos0 = pos0
        self.new_pool, self.new_conf, self.new_gconv, self.new_ffn = [], [], [], []

    def mixer(self, prm, layer):
        if layer % 2 == 0:
            self.x, sp, sc = _even_mixer(self.x, self.states, prm, layer, pos0=self.pos0, **self.tiling)
            self.new_pool.append(sp)
            self.new_conf.append(sc)
        else:
            self.x, sg = _odd_mixer(self.x, self.states, prm, layer, **self.tiling)
            self.new_gconv.append(sg)

    def ffn(self, prm, layer):
        self.x, sf = _conv_ffn(self.x, self.states, prm, layer, **self.tiling)
        self.new_ffn.append(sf)

    def new_states(self):
        new_pool, new_conf = jnp.stack(self.new_pool), jnp.stack(self.new_conf)
        if self.states is None:
            new_pool = new_pool[:, :, POOL_HIST - POOL_BUF:, :]
            new_conf = new_conf[:, :, CONF_HIST - (CONF_WIDTH - 1):, :]
        else:
            new_pool, new_conf = new_pool.transpose(0, 2, 1, 3), new_conf.transpose(0, 2, 1, 3)
        return new_pool, new_conf, jnp.stack(self.new_gconv), jnp.stack(self.new_ffn)


def _cache_rows(c):
    d, n = c.shape[0], c.shape[1]
    c = c.reshape(d, n, N_MEM, N_MEM_HEADS, HALF_HEAD, LANES).transpose(0, 1, 2, 4, 3, 5)
    return c.reshape(d, n, N_MEM * MEM_ROWS, LANES)


def _cache_from_rows(c, n):
    c = c.reshape(DEPTH, n, N_MEM, HALF_HEAD, N_MEM_HEADS, LANES).transpose(0, 1, 2, 4, 3, 5)
    return c.reshape(DEPTH, n, N_MEM, N_MEM_HEADS, MEM_HEAD_DIM)


def kernel(x_prompt, x_sample, state_pool, state_conf, state_gconv, state_ffn, cache_mem_k, cache_mem_v, mem_prompt, norm_gains, w_in_even, w_pool, pool_scale, conf_w, conf_b, conf_ln_g, conf_ln_b, w_out_even, w_in_odd, gconv_w, w_out_odd, w_mem_q, w_mem_k, w_mem_v, w_mem_o, w_ffn_gate, w_ffn_up, ffn_conv_w, ffn_conv_b, w_ffn_down):
    prm = dict(
        gains=norm_gains.transpose(1, 0, 2),
        conf_w=conf_w.transpose(1, 0, 2),
        gconv_w=gconv_w.transpose(1, 0, 2),
        ffn_conv_w=ffn_conv_w.transpose(1, 0, 2),
        pool_scale=pool_scale, conf_b=conf_b, conf_ln_g=conf_ln_g, conf_ln_b=conf_ln_b, ffn_conv_b=ffn_conv_b,
        w_in_even=w_in_even.astype(BF16), w_pool=w_pool.astype(BF16), w_out_even=w_out_even.astype(BF16),
        w_in_odd=w_in_odd.astype(BF16), w_out_odd=w_out_odd.astype(BF16),
        w_mem_q=w_mem_q.astype(BF16), w_mem_o=w_mem_o.astype(BF16),
        w_ffn_gate=w_ffn_gate.astype(BF16), w_ffn_up=w_ffn_up.astype(BF16), w_ffn_down=w_ffn_down.astype(BF16))

    k32, v32, kb, vb = _mem_kv(mem_prompt.reshape(BATCH * N_MEM, D_MODEL), prm["gains"],
                               w_mem_k.astype(BF16), w_mem_v.astype(BF16), tile=PROMPT_TILE)
    kb = kb.reshape(DEPTH * BATCH, N_MEM, D_MODEL)
    vb = vb.reshape(DEPTH * BATCH, N_MEM, D_MODEL)

    prompt = _Group(x_prompt.reshape(BATCH * SEQ, D_MODEL), None, nseq=BATCH, bs=1, ts=PROMPT_TILE, pos0=0)

    states = dict(pool=state_pool.transpose(0, 2, 1, 3), conf=state_conf.transpose(0, 2, 1, 3),
                  gconv=state_gconv, ffn=state_ffn)
    ck, cv = _cache_rows(cache_mem_k), _cache_rows(cache_mem_v)
    sample = _Group(x_sample.reshape(DEC_BATCH * DEC_SEQ, D_MODEL), states, nseq=DEC_BATCH, bs=SAMPLE_SEQS,
                    ts=DEC_SEQ, pos0=PAST_LEN)
    sample_tile = SAMPLE_SEQS * DEC_SEQ

    for l in range(DEPTH):
        prompt.mixer(prm, l)
        sample.mixer(prm, l)
        q_s = _sample_qproj(sample.x, prm, l, tile=sample_tile)
        prompt.x, o_s = _attn_fused(prompt.x, kb, vb, q_s, ck, cv, prm, l, ts=PROMPT_TILE)
        sample.x = _sample_oproj(sample.x, o_s, prm, l, tile=sample_tile)
        prompt.ffn(prm, l)
        sample.ffn(prm, l)

    pool_p, conf_p, gconv_p, ffn_p = prompt.new_states()
    pool_s, conf_s, gconv_s, ffn_s = sample.new_states()
    return (prompt.x.reshape(BATCH, SEQ, D_MODEL), sample.x.reshape(DEC_BATCH, DEC_SEQ, D_MODEL),
            pool_p, conf_p, gconv_p, ffn_p,
            _cache_from_rows(k32.reshape(DEPTH, BATCH, N_MEM * MEM_ROWS, LANES), BATCH),
            _cache_from_rows(v32.reshape(DEPTH, BATCH, N_MEM * MEM_ROWS, LANES), BATCH),
            pool_s, conf_s, gconv_s, ffn_s)
```

```python
import functools

import jax
import jax.numpy as jnp
from jax import lax
from jax.experimental import pallas as pl
from jax.experimental.pallas import tpu as pltpu

D_MODEL = 1024
BATCH = 8
SEQ = 2048
DEPTH = 4
DEC_BATCH = 128
DEC_SEQ = 8
PAST_LEN = 16384
N_EVEN = (DEPTH + 1) // 2
N_ODD = DEPTH // 2
POOL_WINDOWS = (2, 4, 8, 16)
D_POOL = 512
POOL_BUF = 15
D_CONF = 512
CONF_WIDTH = 31
D_GCONV = 1024
GCONV_WIDTH = 3
D_FF = 2816
FFN_CONV_WIDTH = 3
N_MEM = 256
N_MEM_HEADS = 4
MEM_HEAD_DIM = 256
RMS_EPS = 1e-6
LN_EPS = 1e-5
NORM_MIX_PRE, NORM_MIX_POST, NORM_X_PRE, NORM_X_POST, NORM_FFN_PRE, NORM_FFN_POST, NORM_MEM = range(7)

F32 = jnp.float32
BF16 = jnp.bfloat16

LANES = 128
SUBLANES = 8
VMEM_LIMIT_BYTES = 56 * 1024 * 1024

POOL_HIST = 16
CONF_HIST = 32
SHORT_HIST = 8

FFN_CHUNKS = ((0, 1024), (1024, 2048), (2048, 2816))

PROMPT_TILE = 512
SAMPLE_SEQS = 32
CONV_ROWS = 128
MIXER_SUB_TILES = 1

MEM_ROWS = N_MEM_HEADS * MEM_HEAD_DIM // LANES
HALF_HEAD = MEM_HEAD_DIM // LANES


def _rms(x, g):
    ms = jnp.mean(x * x, axis=-1, keepdims=True)
    return x * lax.rsqrt(ms + RMS_EPS) * g


def _mm(a, w):
    return jnp.dot(a.astype(BF16), w, preferred_element_type=F32)


def _lanes(i):
    return slice(i * LANES, (i + 1) * LANES)


def _seq_view(scr, bs):
    n_blk, total, _ = scr.shape
    return scr.reshape(n_blk, bs, total // bs, LANES)


def _carry_history(scr, bs, hist, ts, nj):
    if nj > 1:
        view = _seq_view(scr, bs)

        @pl.when(pl.program_id(1) > 0)
        def _():
            for g in range(scr.shape[0]):
                view[g, :, 0:hist, :] = view[g, :, ts:ts + hist, :]


def _first_tile(fn):
    pl.when(pl.program_id(1) == 0)(fn)


def _cast_weights(pairs, first):
    @pl.when(first)
    def _():
        for w_ref, w_scr in pairs:
            w_scr[...] = w_ref[...].astype(BF16)


def _first_step(n_axes):
    first = pl.program_id(0) == 0
    for axis in range(1, n_axes):
        first = jnp.logical_and(first, pl.program_id(axis) == 0)
    return first


def _load_tm_state(scr, st_ref, bs, hist, n_state, ts):
    rows = hist + ts
    if st_ref is None:
        view = _seq_view(scr, bs)

        def fill():
            for g in range(scr.shape[0]):
                view[g, :, 0:hist, :] = jnp.zeros((bs, hist, LANES), F32)
    else:
        def fill():
            for g in range(scr.shape[0]):
                for r in range(n_state):
                    scr[g, pl.ds(hist - n_state + r, bs, stride=rows), :] = st_ref[r, :, _lanes(g)]
    _first_tile(fill)


def _store_tm_state(ns_ref, scr, bs, hist, n_state, ts):
    rows = hist + ts
    for g in range(scr.shape[0]):
        for r in range(n_state):
            ns_ref[r, :, _lanes(g)] = scr[g, pl.ds(ts + hist - n_state + r, bs, stride=rows), :]


def _store_tail(ns_ref, scr, bs, hist, ts):
    view = _seq_view(scr, bs)
    for g in range(scr.shape[0]):
        ns_ref[:, :, _lanes(g)] = view[g, :, ts:ts + hist, :]


def _load_bm_state(scr, st_ref, bs, hist, n_state):
    view = _seq_view(scr, bs)

    def fill():
        for g in range(scr.shape[0]):
            if st_ref is None:
                view[g, :, hist - n_state:hist, :] = jnp.zeros((bs, n_state, LANES), F32)
            else:
                view[g, :, hist - n_state:hist, :] = st_ref[:, :, _lanes(g)]
    _first_tile(fill)


def _sub_tiles(bs, ts, n_sub):
    if bs == 1:
        return [(0, 1, s * (ts // n_sub), ts // n_sub) for s in range(n_sub)]
    return [(s * (bs // n_sub), bs // n_sub, 0, ts) for s in range(n_sub)]


def _even_kernel(*refs, layer, bs, ts, nj, pos0, has_state, n_sub):
    if has_state:
        x_ref, sp_ref, sc_ref, *refs = refs
    else:
        x_ref, *refs = refs
        sp_ref = sc_ref = None
    (gains_ref, w_in_f32, w_pool_f32, pscale_ref, cw_ref, cb_ref, lng_ref, lnb_ref, w_out_f32,
     xo_ref, npool_ref, nconf_ref, a_scr, u_scr, w_in_ref, w_pool_ref, w_out_ref) = refs
    n_blk = D_POOL // LANES
    e = layer // 2
    _cast_weights([(w_in_f32, w_in_ref), (w_pool_f32, w_pool_ref), (w_out_f32, w_out_ref)], _first_step(2))
    _load_tm_state(a_scr, sp_ref, bs, POOL_HIST, POOL_BUF, ts)
    _load_tm_state(u_scr, sc_ref, bs, CONF_HIST, CONF_WIDTH - 1, ts)
    _carry_history(a_scr, bs, POOL_HIST, ts, nj)
    _carry_history(u_scr, bs, CONF_HIST, ts, nj)
    a_view = _seq_view(a_scr, bs)
    u_view = _seq_view(u_scr, bs)
    base = CONF_HIST - (CONF_WIDTH - 1)

    def project(sub):
        b0, nb, r0, nr = sub
        rows = slice(b0 * ts + r0, b0 * ts + r0 + nb * nr)
        h = _rms(x_ref[rows, :], gains_ref[NORM_MIX_PRE, layer:layer + 1, :])
        p = _mm(h, w_in_ref[...])
        u = p[:, D_POOL:D_POOL + D_CONF] * jax.nn.sigmoid(p[:, D_POOL + D_CONF:])
        for g in range(n_blk):
            a_view[g, b0:b0 + nb, POOL_HIST + r0:POOL_HIST + r0 + nr, :] = p[:, _lanes(g)].reshape(nb, nr, LANES)
            u_view[g, b0:b0 + nb, CONF_HIST + r0:CONF_HIST + r0 + nr, :] = u[:, _lanes(g)].reshape(nb, nr, LANES)

    def mix(sub):
        b0, nb, r0, nr = sub
        mm = nb * nr
        rows = slice(b0 * ts + r0, b0 * ts + r0 + mm)
        pos = pos0 + pl.program_id(1) * ts + r0 + lax.broadcasted_iota(jnp.int32, (1, nr, LANES), 1)
        ya = []
        for g, w in enumerate(POOL_WINDOWS):
            lo = POOL_HIST + r0
            cur = a_view[g, b0:b0 + nb, lo:lo + nr, :]
            win = cur
            for i in range(1, w):
                win = win + a_view[g, b0:b0 + nb, lo - i:lo - i + nr, :]
            inv_cnt = 1.0 / jnp.minimum(pos + 1, w).astype(F32)
            pooled = win * inv_cnt - cur
            ya.append(_mm(pooled.reshape(mm, LANES), w_pool_ref[g]))
        ya = jnp.concatenate(ya, axis=-1) * pscale_ref[e:e + 1, :]

        if nr >= CONV_ROWS:
            bb, rb = 1, CONV_ROWS
        else:
            bb, rb = CONV_ROWS // nr, nr
        cols = []
        for c in range(n_blk):
            blocks = []
            for bi in range(b0, b0 + nb, bb):
                for ri in range(r0, r0 + nr, rb):
                    acc = None
                    for k in range(CONF_WIDTH):
                        term = (u_view[c, bi:bi + bb, base + k + ri:base + k + ri + rb, :]
                                * cw_ref[k, e:e + 1, _lanes(c)].reshape(1, 1, LANES))
                        acc = term if acc is None else acc + term
                    blocks.append(acc.reshape(bb * rb, LANES))
            cols.append(jnp.concatenate(blocks, axis=0))
        cb = jnp.concatenate(cols, axis=-1) + cb_ref[e:e + 1, :]
        mu = jnp.mean(cb, axis=-1, keepdims=True)
        xc = cb - mu
        var = jnp.mean(xc * xc, axis=-1, keepdims=True)
        ln = xc * lax.rsqrt(var + LN_EPS) * lng_ref[e:e + 1, :] + lnb_ref[e:e + 1, :]
        yb = ln * jax.nn.sigmoid(ln)

        y = _mm(jnp.concatenate([ya, yb], axis=-1), w_out_ref[...])
        xo_ref[rows, :] = x_ref[rows, :] + _rms(y, gains_ref[NORM_MIX_POST, layer:layer + 1, :])

    prev = None
    for sub in _sub_tiles(bs, ts, n_sub):
        project(sub)
        if prev is not None:
            mix(prev)
        prev = sub
    mix(prev)

    if has_state:
        _store_tm_state(npool_ref, a_scr, bs, POOL_HIST, POOL_BUF, ts)
        _store_tm_state(nconf_ref, u_scr, bs, CONF_HIST, CONF_WIDTH - 1, ts)
    else:
        _store_tail(npool_ref, a_scr, bs, POOL_HIST, ts)
        _store_tail(nconf_ref, u_scr, bs, CONF_HIST, ts)


def _odd_kernel(*refs, layer, bs, ts, nj, has_state):
    if has_state:
        x_ref, sg_ref, *refs = refs
    else:
        x_ref, *refs = refs
        sg_ref = None
    gains_ref, w_in_f32, gw_ref, w_out_f32, xo_ref, ng_ref, v_scr, w_in_ref, w_out_ref = refs
    m = bs * ts
    n_blk = D_GCONV // LANES
    n_state = GCONV_WIDTH - 1
    o = layer // 2
    _cast_weights([(w_in_f32, w_in_ref), (w_out_f32, w_out_ref)], _first_step(2))
    _load_bm_state(v_scr, sg_ref, bs, SHORT_HIST, n_state)
    _carry_history(v_scr, bs, SHORT_HIST, ts, nj)
    view = _seq_view(v_scr, bs)

    x = x_ref[...]
    h = _rms(x, gains_ref[NORM_MIX_PRE, layer:layer + 1, :])
    p = _mm(h, w_in_ref[...])
    v = p[:, 2 * D_GCONV:] * p[:, :D_GCONV]
    ys = []
    for g in range(n_blk):
        view[g, :, SHORT_HIST:SHORT_HIST + ts, :] = v[:, _lanes(g)].reshape(bs, ts, LANES)
        conv = None
        for k in range(GCONV_WIDTH):
            r = SHORT_HIST - n_state + k
            term = view[g, :, r:r + ts, :] * gw_ref[k, o:o + 1, _lanes(g)].reshape(1, 1, LANES)
            conv = term if conv is None else conv + term
        ys.append(conv.reshape(m, LANES))
    y = p[:, D_GCONV:2 * D_GCONV] * jnp.concatenate(ys, axis=-1)
    out = _mm(y, w_out_ref[...])
    xo_ref[...] = x + _rms(out, gains_ref[NORM_MIX_POST, layer:layer + 1, :])
    for g in range(n_blk):
        ng_ref[:, :, _lanes(g)] = view[g, :, ts + SHORT_HIST - n_state:ts + SHORT_HIST, :]


def _ffn_kernel(*refs, layer, bs, ts, nj, has_state):
    if has_state:
        x_ref, sf_ref, *refs = refs
    else:
        x_ref, *refs = refs
        sf_ref = None
    gains_ref, wg_ref, wu_ref, cw_ref, cb_ref, wd_ref, xo_ref, nf_ref, g_scr = refs
    m = bs * ts
    n_blk = D_FF // LANES
    n_state = FFN_CONV_WIDTH - 1
    _load_bm_state(g_scr, sf_ref, bs, SHORT_HIST, n_state)
    _carry_history(g_scr, bs, SHORT_HIST, ts, nj)
    view = _seq_view(g_scr, bs)

    x = x_ref[...]
    h = _rms(x, gains_ref[NORM_FFN_PRE, layer:layer + 1, :]).astype(BF16)
    acc = None
    for c0, c1 in FFN_CHUNKS:
        gate = jnp.dot(h, wg_ref[:, c0:c1], preferred_element_type=F32)
        up = jnp.dot(h, wu_ref[:, c0:c1], preferred_element_type=F32)
        gcs = []
        for l in range(c0 // LANES, c1 // LANES):
            off = l * LANES - c0
            view[l, :, SHORT_HIST:SHORT_HIST + ts, :] = gate[:, off:off + LANES].reshape(bs, ts, LANES)
            conv = None
            for k in range(FFN_CONV_WIDTH):
                r = SHORT_HIST - n_state + k
                term = view[l, :, r:r + ts, :] * cw_ref[k, layer:layer + 1, _lanes(l)].reshape(1, 1, LANES)
                conv = term if conv is None else conv + term
            gcs.append(conv.reshape(m, LANES))
        gc = jnp.concatenate(gcs, axis=-1) + cb_ref[layer:layer + 1, c0:c1]
        act = gc * jax.nn.sigmoid(gc) * up
        part = _mm(act, wd_ref[c0:c1, :])
        acc = part if acc is None else acc + part
    xo_ref[...] = x + _rms(acc, gains_ref[NORM_FFN_POST, layer:layer + 1, :])
    for l in range(n_blk):
        nf_ref[:, :, _lanes(l)] = view[l, :, ts + SHORT_HIST - n_state:ts + SHORT_HIST, :]


def _softmax_times(s, v):
    e = jnp.exp(s - jnp.max(s, axis=-1, keepdims=True))
    return _mm(e, v) * (1.0 / jnp.sum(e, axis=-1, keepdims=True))


ATTN_SCALE = MEM_HEAD_DIM ** -0.5


def _head_cols(hd):
    return slice(hd * MEM_HEAD_DIM, (hd + 1) * MEM_HEAD_DIM)


def _kv_kernel(mem_ref, gains_ref, wk_f32, wv_f32, k32_ref, v32_ref, kb_ref, vb_ref, wk_ref, wv_ref):
    layer = pl.program_id(0)
    rows = mem_ref.shape[0]
    _cast_weights([(wk_f32, wk_ref), (wv_f32, wv_ref)], pl.program_id(1) == 0)
    mn = _rms(mem_ref[...], gains_ref[NORM_MEM, pl.ds(layer, 1), :]).astype(BF16)
    for w_ref, o32_ref, ob_ref in ((wk_ref, k32_ref, kb_ref), (wv_ref, v32_ref, vb_ref)):
        kv = jnp.dot(mn, w_ref[...], preferred_element_type=F32)
        ob_ref[...] = kv.astype(BF16)
        for hd in range(N_MEM_HEADS):
            for half in range(HALF_HEAD):
                c0 = hd * MEM_HEAD_DIM + half * LANES
                o32_ref[pl.ds(half * N_MEM_HEADS + hd, rows, stride=MEM_ROWS), :] = kv[:, c0:c0 + LANES]


def _qproj_kernel(x_ref, gains_ref, wq_f32, q_ref, wq_ref, *, layer):
    _cast_weights([(wq_f32, wq_ref)], _first_step(1))
    h = _rms(x_ref[...], gains_ref[NORM_X_PRE, layer:layer + 1, :])
    q_ref[...] = _mm(h, wq_ref[...]) * ATTN_SCALE


def _cache_flat(c_ref, b):
    blocks = [c_ref[b, pl.ds(half * N_MEM_HEADS + hd, N_MEM, stride=MEM_ROWS), :]
              for hd in range(N_MEM_HEADS) for half in range(HALF_HEAD)]
    return jnp.concatenate(blocks, axis=-1).astype(BF16)


def _cached_attention(q_ref, k_ref, v_ref, o_ref, ts):
    zero = jnp.zeros((ts, MEM_HEAD_DIM), F32)

    def scores(b):
        q = q_ref[b * ts:(b + 1) * ts, :]
        q_heads = jnp.concatenate(
            [jnp.concatenate([q[:, _head_cols(hd)] if hd == blk else zero for hd in range(N_MEM_HEADS)], axis=-1)
             for blk in range(N_MEM_HEADS)], axis=0).astype(BF16)
        return lax.dot_general(q_heads, _cache_flat(k_ref, b), (((1,), (1,)), ((), ())),
                               preferred_element_type=F32)

    def finish(b, s):
        o_all = _softmax_times(s, _cache_flat(v_ref, b))
        o_ref[b * ts:(b + 1) * ts, :] = jnp.concatenate(
            [o_all[hd * ts:(hd + 1) * ts, _head_cols(hd)] for hd in range(N_MEM_HEADS)], axis=-1)

    return scores, finish


def _attn_fused_kernel(x_ref, k_ref, v_ref, gains_ref, wq_f32, wo_f32, qs_ref, ck_ref, cv_ref, xo_ref, os_ref,
                       wq_ref, wo_ref, *, layer, bs, ts):
    _cast_weights([(wq_f32, wq_ref), (wo_f32, wo_ref)], _first_step(2))
    scores, finish = _cached_attention(qs_ref, ck_ref, cv_ref, os_ref, ts)
    x = x_ref[...]
    h = _rms(x, gains_ref[NORM_X_PRE, layer:layer + 1, :])
    q = (_mm(h, wq_ref[...]) * ATTN_SCALE).astype(BF16)
    outs = []
    pending = None
    for step in range(max(N_MEM_HEADS, bs)):
        s_sample = scores(step) if step < bs else None
        if step < N_MEM_HEADS:
            cols = _head_cols(step)
            s = lax.dot_general(q[:, cols], k_ref[:, cols], (((1,), (1,)), ((), ())), preferred_element_type=F32)
            outs.append(_softmax_times(s, v_ref[:, cols]))
        if pending is not None:
            finish(*pending)
        pending = (step, s_sample) if s_sample is not None else None
    if pending is not None:
        finish(*pending)
    y = _mm(jnp.concatenate(outs, axis=-1), wo_ref[...])
    xo_ref[...] = x + _rms(y, gains_ref[NORM_X_POST, layer:layer + 1, :])


def _oproj_kernel(x_ref, o_ref, gains_ref, wo_f32, xo_ref, wo_ref, *, layer):
    _cast_weights([(wo_f32, wo_ref)], _first_step(1))
    y = _mm(o_ref[...], wo_ref[...])
    xo_ref[...] = x_ref[...] + _rms(y, gains_ref[NORM_X_POST, layer:layer + 1, :])


def _params(n_axes):
    return pltpu.CompilerParams(dimension_semantics=("arbitrary",) * n_axes,
                                vmem_limit_bytes=VMEM_LIMIT_BYTES)


def _whole(arr):
    nd = arr.ndim
    return pl.BlockSpec(arr.shape, lambda *_: (0,) * nd)


def _layer_weight(arr, layer):
    nd = arr.ndim
    return pl.BlockSpec((None,) + arr.shape[1:], lambda *_: (layer,) + (0,) * (nd - 1),
                        pipeline_mode=pl.Buffered(1))


def _bf16_copy(arr):
    return pltpu.VMEM(arr.shape[1:], BF16)


def _row_spec(m, nj, width):
    return pl.BlockSpec((m, width), lambda b, j: (b * nj + j, 0))


def _tm_state_in(idx, n_state, bs, width):
    return pl.BlockSpec((None, n_state, bs, width), lambda b, j: (idx, 0, b, 0))


def _tm_state_out(n_state, bs, width):
    return pl.BlockSpec((n_state, bs, width), lambda b, j: (0, b, 0))


def _bm_state_in(idx, n_state, bs, width):
    return pl.BlockSpec((None, bs, n_state, width), lambda b, j: (idx, b, 0, 0))


def _bm_state_out(n_state, bs, width):
    return pl.BlockSpec((bs, n_state, width), lambda b, j: (b, 0, 0))


def _even_mixer(x, states, prm, layer, *, nseq, bs, ts, pos0):
    nb, nj = nseq // bs, x.shape[0] // (nseq * ts)
    m = bs * ts
    e = layer // 2
    has_state = states is not None
    n_sub = MIXER_SUB_TILES if bs == 1 else 1
    kern = functools.partial(_even_kernel, layer=layer, bs=bs, ts=ts, nj=nj, pos0=pos0, has_state=has_state,
                             n_sub=n_sub)
    small = [prm["gains"], prm["pool_scale"], prm["conf_w"], prm["conf_b"], prm["conf_ln_g"], prm["conf_ln_b"]]
    args = [x]
    in_specs = [_row_spec(m, nj, D_MODEL)]
    if has_state:
        args += [states["pool"], states["conf"]]
        in_specs += [_tm_state_in(e, POOL_BUF, bs, D_POOL), _tm_state_in(e, CONF_WIDTH - 1, bs, D_CONF)]
    args += [small[0], prm["w_in_even"], prm["w_pool"], *small[1:], prm["w_out_even"]]
    in_specs += [_whole(small[0]), _layer_weight(prm["w_in_even"], e), _layer_weight(prm["w_pool"], e),
                 *[_whole(a) for a in small[1:]], _layer_weight(prm["w_out_even"], e)]
    if has_state:
        state_specs = [_tm_state_out(POOL_BUF, bs, D_POOL), _tm_state_out(CONF_WIDTH - 1, bs, D_CONF)]
        state_shapes = [(POOL_BUF, nseq, D_POOL), (CONF_WIDTH - 1, nseq, D_CONF)]
    else:
        state_specs = [_bm_state_out(POOL_HIST, bs, D_POOL), _bm_state_out(CONF_HIST, bs, D_CONF)]
        state_shapes = [(nseq, POOL_HIST, D_POOL), (nseq, CONF_HIST, D_CONF)]
    return pl.pallas_call(
        kern,
        grid=(nb, nj),
        in_specs=in_specs,
        out_specs=[_row_spec(m, nj, D_MODEL), *state_specs],
        out_shape=[jax.ShapeDtypeStruct(x.shape, F32), *[jax.ShapeDtypeStruct(s, F32) for s in state_shapes]],
        scratch_shapes=[pltpu.VMEM((D_POOL // LANES, bs * (POOL_HIST + ts), LANES), F32),
                        pltpu.VMEM((D_CONF // LANES, bs * (CONF_HIST + ts), LANES), F32),
                        _bf16_copy(prm["w_in_even"]), _bf16_copy(prm["w_pool"]), _bf16_copy(prm["w_out_even"])],
        compiler_params=_params(2),
        name="even_mixer",
    )(*args)


def _odd_mixer(x, states, prm, layer, *, nseq, bs, ts):
    nb, nj = nseq // bs, x.shape[0] // (nseq * ts)
    m = bs * ts
    o = layer // 2
    has_state = states is not None
    kern = functools.partial(_odd_kernel, layer=layer, bs=bs, ts=ts, nj=nj, has_state=has_state)
    args = [x]
    in_specs = [_row_spec(m, nj, D_MODEL)]
    if has_state:
        args.append(states["gconv"])
        in_specs.append(_bm_state_in(o, GCONV_WIDTH - 1, bs, D_GCONV))
    args += [prm["gains"], prm["w_in_odd"], prm["gconv_w"], prm["w_out_odd"]]
    in_specs += [_whole(prm["gains"]), _layer_weight(prm["w_in_odd"], o), _whole(prm["gconv_w"]),
                 _layer_weight(prm["w_out_odd"], o)]
    return pl.pallas_call(
        kern,
        grid=(nb, nj),
        in_specs=in_specs,
        out_specs=[_row_spec(m, nj, D_MODEL), _bm_state_out(GCONV_WIDTH - 1, bs, D_GCONV)],
        out_shape=[jax.ShapeDtypeStruct(x.shape, F32),
                   jax.ShapeDtypeStruct((nseq, GCONV_WIDTH - 1, D_GCONV), F32)],
        scratch_shapes=[pltpu.VMEM((D_GCONV // LANES, bs * (SHORT_HIST + ts), LANES), F32),
                        _bf16_copy(prm["w_in_odd"]), _bf16_copy(prm["w_out_odd"])],
        compiler_params=_params(2),
        name="odd_mixer",
    )(*args)


def _conv_ffn(x, states, prm, layer, *, nseq, bs, ts):
    nb, nj = nseq // bs, x.shape[0] // (nseq * ts)
    m = bs * ts
    has_state = states is not None
    kern = functools.partial(_ffn_kernel, layer=layer, bs=bs, ts=ts, nj=nj, has_state=has_state)
    args = [x]
    in_specs = [_row_spec(m, nj, D_MODEL)]
    if has_state:
        args.append(states["ffn"])
        in_specs.append(_bm_state_in(layer, FFN_CONV_WIDTH - 1, bs, D_FF))
    args += [prm["gains"], prm["w_ffn_gate"], prm["w_ffn_up"], prm["ffn_conv_w"], prm["ffn_conv_b"],
             prm["w_ffn_down"]]
    in_specs += [_whole(prm["gains"]), _layer_weight(prm["w_ffn_gate"], layer),
                 _layer_weight(prm["w_ffn_up"], layer), _whole(prm["ffn_conv_w"]), _whole(prm["ffn_conv_b"]),
                 _layer_weight(prm["w_ffn_down"], layer)]
    return pl.pallas_call(
        kern,
        grid=(nb, nj),
        in_specs=in_specs,
        out_specs=[_row_spec(m, nj, D_MODEL), _bm_state_out(FFN_CONV_WIDTH - 1, bs, D_FF)],
        out_shape=[jax.ShapeDtypeStruct(x.shape, F32),
                   jax.ShapeDtypeStruct((nseq, FFN_CONV_WIDTH - 1, D_FF), F32)],
        scratch_shapes=[pltpu.VMEM((D_FF // LANES, bs * (SHORT_HIST + ts), LANES), F32)],
        compiler_params=_params(2),
        name="conv_ffn",
    )(*args)


def _attn_fused(x, kb, vb, q_s, ck, cv, prm, layer, *, ts):
    nj = SEQ // ts
    bs = DEC_BATCH // (BATCH * nj)
    kv_spec = pl.BlockSpec((None, N_MEM, D_MODEL), lambda b, j: (layer * BATCH + b, 0, 0))
    sample_rows = _row_spec(bs * DEC_SEQ, nj, D_MODEL)
    cache_spec = pl.BlockSpec((None, bs, N_MEM * MEM_ROWS, LANES), lambda b, j: (layer, b * nj + j, 0, 0))
    return pl.pallas_call(
        functools.partial(_attn_fused_kernel, layer=layer, bs=bs, ts=DEC_SEQ),
        grid=(BATCH, nj),
        in_specs=[_row_spec(ts, nj, D_MODEL), kv_spec, kv_spec, _whole(prm["gains"]),
                  _layer_weight(prm["w_mem_q"], layer), _layer_weight(prm["w_mem_o"], layer),
                  sample_rows, cache_spec, cache_spec],
        out_specs=[_row_spec(ts, nj, D_MODEL), sample_rows],
        out_shape=[jax.ShapeDtypeStruct(x.shape, F32), jax.ShapeDtypeStruct(q_s.shape, F32)],
        scratch_shapes=[_bf16_copy(prm["w_mem_q"]), _bf16_copy(prm["w_mem_o"])],
        compiler_params=_params(2),
        name="attn_fused",
    )(x, kb, vb, prm["gains"], prm["w_mem_q"], prm["w_mem_o"], q_s, ck, cv)


def _mem_kv(mem, gains, wk, wv, *, tile):
    rows = mem.shape[0]
    w_spec = pl.BlockSpec((None, D_MODEL, D_MODEL), lambda l, i: (l, 0, 0))
    o32 = pl.BlockSpec((None, tile * MEM_ROWS, LANES), lambda l, i: (l, i, 0))
    ob = pl.BlockSpec((None, tile, D_MODEL), lambda l, i: (l, i, 0))
    return pl.pallas_call(
        _kv_kernel,
        grid=(DEPTH, rows // tile),
        in_specs=[pl.BlockSpec((tile, D_MODEL), lambda l, i: (i, 0)), _whole(gains), w_spec, w_spec],
        out_specs=[o32, o32, ob, ob],
        out_shape=[jax.ShapeDtypeStruct((DEPTH, rows * MEM_ROWS, LANES), F32)] * 2
        + [jax.ShapeDtypeStruct((DEPTH, rows, D_MODEL), BF16)] * 2,
        scratch_shapes=[_bf16_copy(wk), _bf16_copy(wv)],
        compiler_params=_params(2),
        name="mem_kv",
    )(mem, gains, wk, wv)


def _sample_qproj(x, prm, layer, *, tile):
    row = pl.BlockSpec((tile, D_MODEL), lambda i: (i, 0))
    return pl.pallas_call(
        functools.partial(_qproj_kernel, layer=layer),
        grid=(x.shape[0] // tile,),
        in_specs=[row, _whole(prm["gains"]), _layer_weight(prm["w_mem_q"], layer)],
        out_specs=row,
        out_shape=jax.ShapeDtypeStruct(x.shape, F32),
        scratch_shapes=[_bf16_copy(prm["w_mem_q"])],
        compiler_params=_params(1),
        name="attn_q",
    )(x, prm["gains"], prm["w_mem_q"])


def _sample_oproj(x, o, prm, layer, *, tile):
    row = pl.BlockSpec((tile, D_MODEL), lambda i: (i, 0))
    return pl.pallas_call(
        functools.partial(_oproj_kernel, layer=layer),
        grid=(x.shape[0] // tile,),
        in_specs=[row, row, _whole(prm["gains"]), _layer_weight(prm["w_mem_o"], layer)],
        out_specs=row,
        out_shape=jax.ShapeDtypeStruct(x.shape, F32),
        scratch_shapes=[_bf16_copy(prm["w_mem_o"])],
        compiler_params=_params(1),
        name="attn_o",
    )(x, o, prm["gains"], prm["w_mem_o"])


class _Group:
    def __init__(self, x, states, *, nseq, bs, ts, pos0):
        self.x, self.states = x, states
        self.tiling = dict(nseq=nseq, bs=bs, ts=ts)
        self.pos0 = pos0
        self.new_pool, self.new_conf, self.new_gconv, self.new_ffn = [], [], [], []

    def mixer(self, prm, layer):
        if layer % 2 == 0:
            self.x, sp, sc = _even_mixer(self.x, self.states, prm, layer, pos0=self.pos0, **self.tiling)
            self.new_pool.append(sp)
            self.new_conf.append(sc)
        else:
            self.x, sg = _odd_mixer(self.x, self.states, prm, layer, **self.tiling)
            self.new_gconv.append(sg)

    def ffn(self, prm, layer):
        self.x, sf = _conv_ffn(self.x, self.states, prm, layer, **self.tiling)
        self.new_ffn.append(sf)

    def new_states(self):
        new_pool, new_conf = jnp.stack(self.new_pool), jnp.stack(self.new_conf)
        if self.states is None:
            new_pool = new_pool[:, :, POOL_HIST - POOL_BUF:, :]
            new_conf = new_conf[:, :, CONF_HIST - (CONF_WIDTH - 1):, :]
        else:
            new_pool, new_conf = new_pool.transpose(0, 2, 1, 3), new_conf.transpose(0, 2, 1, 3)
        return new_pool, new_conf, jnp.stack(self.new_gconv), jnp.stack(self.new_ffn)


def _cache_rows(c):
    d, n = c.shape[0], c.shape[1]
    c = c.reshape(d, n, N_MEM, N_MEM_HEADS, HALF_HEAD, LANES).transpose(0, 1, 2, 4, 3, 5)
    return c.reshape(d, n, N_MEM * MEM_ROWS, LANES)


def _cache_from_rows(c, n):
    c = c.reshape(DEPTH, n, N_MEM, HALF_HEAD, N_MEM_HEADS, LANES).transpose(0, 1, 2, 4, 3, 5)
    return c.reshape(DEPTH, n, N_MEM, N_MEM_HEADS, MEM_HEAD_DIM)


def kernel(x_prompt, x_sample, state_pool, state_conf, state_gconv, state_ffn, cache_mem_k, cache_mem_v, mem_prompt, norm_gains, w_in_even, w_pool, pool_scale, conf_w, conf_b, conf_ln_g, conf_ln_b, w_out_even, w_in_odd, gconv_w, w_out_odd, w_mem_q, w_mem_k, w_mem_v, w_mem_o, w_ffn_gate, w_ffn_up, ffn_conv_w, ffn_conv_b, w_ffn_down):
    prm = dict(
        gains=norm_gains.transpose(1, 0, 2),
        conf_w=conf_w.transpose(1, 0, 2),
        gconv_w=gconv_w.transpose(1, 0, 2),
        ffn_conv_w=ffn_conv_w.transpose(1, 0, 2),
        pool_scale=pool_scale, conf_b=conf_b, conf_ln_g=conf_ln_g, conf_ln_b=conf_ln_b, ffn_conv_b=ffn_conv_b,
        w_in_even=w_in_even, w_pool=w_pool, w_out_even=w_out_even, w_in_odd=w_in_odd, w_out_odd=w_out_odd,
        w_mem_q=w_mem_q, w_mem_o=w_mem_o,
        w_ffn_gate=w_ffn_gate.astype(BF16), w_ffn_up=w_ffn_up.astype(BF16), w_ffn_down=w_ffn_down.astype(BF16))

    k32, v32, kb, vb = _mem_kv(mem_prompt.reshape(BATCH * N_MEM, D_MODEL), prm["gains"],
                               w_mem_k, w_mem_v, tile=PROMPT_TILE)
    kb = kb.reshape(DEPTH * BATCH, N_MEM, D_MODEL)
    vb = vb.reshape(DEPTH * BATCH, N_MEM, D_MODEL)

    prompt = _Group(x_prompt.reshape(BATCH * SEQ, D_MODEL), None, nseq=BATCH, bs=1, ts=PROMPT_TILE, pos0=0)

    states = dict(pool=state_pool.transpose(0, 2, 1, 3), conf=state_conf.transpose(0, 2, 1, 3),
                  gconv=state_gconv, ffn=state_ffn)
    ck, cv = _cache_rows(cache_mem_k), _cache_rows(cache_mem_v)
    sample = _Group(x_sample.reshape(DEC_BATCH * DEC_SEQ, D_MODEL), states, nseq=DEC_BATCH, bs=SAMPLE_SEQS,
                    ts=DEC_SEQ, pos0=PAST_LEN)
    sample_tile = SAMPLE_SEQS * DEC_SEQ

    for l in range(DEPTH):
        prompt.mixer(prm, l)
        sample.mixer(prm, l)
        q_s = _sample_qproj(sample.x, prm, l, tile=sample_tile)
        prompt.x, o_s = _attn_fused(prompt.x, kb, vb, q_s, ck, cv, prm, l, ts=PROMPT_TILE)
        sample.x = _sample_oproj(sample.x, o_s, prm, l, tile=sample_tile)
        prompt.ffn(prm, l)
        sample.ffn(prm, l)

    pool_p, conf_p, gconv_p, ffn_p = prompt.new_states()
    pool_s, conf_s, gconv_s, ffn_s = sample.new_states()
    return (prompt.x.reshape(BATCH, SEQ, D_MODEL), sample.x.reshape(DEC_BATCH, DEC_SEQ, D_MODEL),
            pool_p, conf_p, gconv_p, ffn_p,
            _cache_from_rows(k32.reshape(DEPTH, BATCH, N_MEM * MEM_ROWS, LANES), BATCH),
            _cache_from_rows(v32.reshape(DEPTH, BATCH, N_MEM * MEM_ROWS, LANES), BATCH),
            pool_s, conf_s, gconv_s, ffn_s)
```
